```python
import jax, jax.numpy as jnp
from jax import lax
import numpy as np

D_MODEL = 4096
BATCH = 4
SEQ = 4096
DEPTH = 4

CTX_LEN = 256
GRID_W = 64
N_MIXERS = 3
NORM_EPS = 1e-6
ADA_RANK = 256
N_MOD = 6
MLA_HEADS = 64
MLA_Q_RANK = 1536
MLA_KV_RANK = 512
MLA_NOPE_DIM = 128
MLA_ROPE_DIM = 64
MLA_V_DIM = 128
MLA_SCALE = (MLA_NOPE_DIM + MLA_ROPE_DIM) ** -0.5
ROPE_FREQS = MLA_ROPE_DIM // 4
ROPE_THETA = 10000.0
Q_BLOCK = 128
GM_WIDTH = D_MODEL
GM_GROUPS = 8
GM_CHUNK = 128
FN_GROUPS = 8
FFN_DIM = 11008
N_MLA = (DEPTH + N_MIXERS - 1) // N_MIXERS
N_GM = (DEPTH + N_MIXERS - 2) // N_MIXERS
N_FN = DEPTH // N_MIXERS

kernel_name = "hybrid_mla_gmlp_fnet_convglu_dit"


def rmsnorm(x, g):
    xf = x.astype(jnp.float32)
    y = xf * lax.rsqrt(jnp.mean(xf * xf, axis=-1, keepdims=True) + NORM_EPS)
    return (y * g.astype(jnp.float32)).astype(x.dtype)


def layernorm(x, g, b):
    xf = x.astype(jnp.float32)
    mu = jnp.mean(xf, axis=-1, keepdims=True)
    d = xf - mu
    y = d * lax.rsqrt(jnp.mean(d * d, axis=-1, keepdims=True) + NORM_EPS)
    return (y * g.astype(jnp.float32) + b.astype(jnp.float32)).astype(x.dtype)


def ada_mod(cvec, a, b, bias):
    return (jax.nn.silu(cvec) @ a) @ b + bias


def modulate(x, g, shift, scale):
    return rmsnorm(x, g) * (1.0 + scale) + shift


def axial_angles(n_tok):
    rows = n_tok // GRID_W
    row = jnp.broadcast_to(jnp.arange(rows)[:, None], (rows, GRID_W)).reshape(-1).astype(jnp.float32)
    col = jnp.broadcast_to(jnp.arange(GRID_W)[None, :], (rows, GRID_W)).reshape(-1).astype(jnp.float32)
    inv = ROPE_THETA ** (-jnp.arange(ROPE_FREQS, dtype=jnp.float32) / ROPE_FREQS)
    return jnp.stack([row[:, None] * inv, col[:, None] * inv], axis=1)


def axial_rope(x, ang):
    xr = x.astype(jnp.float32).reshape(x.shape[:-1] + (2, 2, ROPE_FREQS))
    x1, x2 = xr[..., 0, :], xr[..., 1, :]
    cos, sin = jnp.cos(ang), jnp.sin(ang)
    out = jnp.stack([x1 * cos - x2 * sin, x2 * cos + x1 * sin], axis=-2)
    return out.reshape(x.shape).astype(x.dtype)


def mla_queries(h, w_dq, q_norm_g, w_uq, qn_nope_g, qn_rope_g):
    bsz, n, _ = h.shape
    cq = rmsnorm(h @ w_dq, q_norm_g)
    q = (cq @ w_uq).reshape(bsz, n, MLA_HEADS, MLA_NOPE_DIM + MLA_ROPE_DIM)
    return rmsnorm(q[..., :MLA_NOPE_DIM], qn_nope_g), rmsnorm(q[..., MLA_NOPE_DIM:], qn_rope_g)


def mla_keys(h, w_dkv, kv_norm_g, w_ukv, kn_nope_g, kn_rope_g):
    bsz, n, _ = h.shape
    kv = h @ w_dkv
    ckv = rmsnorm(kv[..., :MLA_KV_RANK], kv_norm_g)
    k_rope = rmsnorm(kv[..., MLA_KV_RANK:], kn_rope_g)
    kvu = (ckv @ w_ukv).reshape(bsz, n, MLA_HEADS, MLA_NOPE_DIM + MLA_V_DIM)
    k_nope = rmsnorm(kvu[..., :MLA_NOPE_DIM], kn_nope_g)
    return k_nope, k_rope, kvu[..., MLA_NOPE_DIM:]


def mla_attend(qn, qr, kn, kr, v):
    s = (jnp.einsum('bqhd,bkhd->bhqk', qn, kn)
         + jnp.einsum('bqhr,bkr->bhqk', qr, kr)).astype(jnp.float32) * MLA_SCALE
    p = jax.nn.softmax(s, axis=-1).astype(v.dtype)
    return jnp.einsum('bhqk,bkhd->bqhd', p, v)


def mla_mixer(hl, hc, ang, w_dq, q_norm_g, w_uq, w_dkv, kv_norm_g, w_ukv,
              qn_nope_g, qn_rope_g, kn_nope_g, kn_rope_g, w_o, ctx_out):
    bsz, n_lat, _ = hl.shape
    kn_c, kr_c, v_c = mla_keys(hc, w_dkv, kv_norm_g, w_ukv, kn_nope_g, kn_rope_g)
    kn_l, kr_l, v_l = mla_keys(hl, w_dkv, kv_norm_g, w_ukv, kn_nope_g, kn_rope_g)
    kr_l = axial_rope(kr_l, ang)
    qn_l, qr_l = mla_queries(hl, w_dq, q_norm_g, w_uq, qn_nope_g, qn_rope_g)
    qr_l = axial_rope(qr_l, ang[:, None])
    kn = jnp.concatenate([kn_c, kn_l], axis=1)
    kr = jnp.concatenate([kr_c, kr_l], axis=1)
    v = jnp.concatenate([v_c, v_l], axis=1)
    nb = n_lat // Q_BLOCK

    def blocks(a):
        return jnp.moveaxis(a.reshape((bsz, nb, Q_BLOCK) + a.shape[2:]), 1, 0)

    o = lax.map(lambda qs: mla_attend(qs[0], qs[1], kn, kr, v), (blocks(qn_l), blocks(qr_l)))
    yl = jnp.moveaxis(o, 0, 1).reshape(bsz, n_lat, MLA_HEADS * MLA_V_DIM) @ w_o
    yc = None
    if ctx_out:
        qn_c, qr_c = mla_queries(hc, w_dq, q_norm_g, w_uq, qn_nope_g, qn_rope_g)
        oc = mla_attend(qn_c, qr_c, kn_c, kr_c, v_c)
        yc = oc.reshape(bsz, hc.shape[1], MLA_HEADS * MLA_V_DIM) @ w_o
    return yl, yc


def chunk_gmlp(h, w_in, b_in, v_norm_g, v_norm_b, w_s, b_s, w_out):
    bsz, n, _ = h.shape
    z = jax.nn.gelu(h @ w_in + b_in)
    u, v = z[..., :GM_WIDTH], z[..., GM_WIDTH:]
    v = layernorm(v, v_norm_g, v_norm_b)
    v = v.reshape(bsz, n // GM_CHUNK, GM_CHUNK, GM_GROUPS, GM_WIDTH // GM_GROUPS)
    sv = jnp.einsum('gpq,bcqgd->bcpgd', w_s, v) + b_s.T[:, :, None]
    return (u * sv.reshape(bsz, n, GM_WIDTH)) @ w_out


def fourier_mix(h, w_f, b_f):
    bsz, n, d = h.shape
    hg = h.astype(jnp.float32).reshape(bsz, n, FN_GROUPS, d // FN_GROUPS)
    f = jnp.fft.fft2(hg, axes=(1, 3), norm='ortho').real.astype(h.dtype).reshape(bsz, n, d)
    return f @ w_f + b_f


def dwconv3(x, w, b):
    xp = jnp.pad(x, ((0, 0), (1, 1), (0, 0)))
    return xp[:, :-2] * w[0] + xp[:, 1:-1] * w[1] + xp[:, 2:] * w[2] + b


def conv_ffn(h, w_gate, w_val, conv_w, conv_b, w_down):
    g = dwconv3(h @ w_gate, conv_w, conv_b)
    return (jax.nn.silu(g) * (h @ w_val)) @ w_down


def setup_inputs(seed: int = 0) -> dict:
    key = jax.random.key(seed)
    ks = iter(jax.random.split(key, 48))

    def nrm(shape, scale):
        return jax.random.normal(next(ks), shape, jnp.float32) * scale

    def gain(shape):
        return 1.0 + 0.02 * jax.random.normal(next(ks), shape, jnp.float32)

    D, F, R, H = D_MODEL, FFN_DIM, ADA_RANK, MLA_HEADS
    return {
        "x": nrm((BATCH, SEQ, D), 1.0),
        "c": nrm((BATCH, D), 1.0),
        "ctx": nrm((BATCH, CTX_LEN, D), 1.0),
        "c_ctx": nrm((D,), 1.0),
        "norm1_g": gain((DEPTH, D)),
        "norm2_g": gain((DEPTH, D)),
        "ada_a": nrm((DEPTH, D, R), D ** -0.5),
        "ada_b": nrm((DEPTH, R, N_MOD * D), 0.5 * R ** -0.5),
        "ada_bias": nrm((DEPTH, N_MOD * D), 0.02),
        "ffn_w_gate": nrm((DEPTH, D, F), D ** -0.5),
        "ffn_w_val": nrm((DEPTH, D, F), D ** -0.5),
        "ffn_conv_w": nrm((DEPTH, 3, F), 3 ** -0.5),
        "ffn_conv_b": nrm((DEPTH, F), 0.02),
        "ffn_w_down": nrm((DEPTH, F, D), F ** -0.5),
        "mla_w_dq": nrm((N_MLA, D, MLA_Q_RANK), D ** -0.5),
        "mla_q_norm_g": gain((N_MLA, MLA_Q_RANK)),
        "mla_w_uq": nrm((N_MLA, MLA_Q_RANK, H * (MLA_NOPE_DIM + MLA_ROPE_DIM)), MLA_Q_RANK ** -0.5),
        "mla_w_dkv": nrm((N_MLA, D, MLA_KV_RANK + MLA_ROPE_DIM), D ** -0.5),
        "mla_kv_norm_g": gain((N_MLA, MLA_KV_RANK)),
        "mla_w_ukv": nrm((N_MLA, MLA_KV_RANK, H * (MLA_NOPE_DIM + MLA_V_DIM)), MLA_KV_RANK ** -0.5),
        "mla_qn_nope_g": gain((N_MLA, MLA_NOPE_DIM)),
        "mla_qn_rope_g": gain((N_MLA, MLA_ROPE_DIM)),
        "mla_kn_nope_g": gain((N_MLA, MLA_NOPE_DIM)),
        "mla_kn_rope_g": gain((N_MLA, MLA_ROPE_DIM)),
        "mla_w_o": nrm((N_MLA, H * MLA_V_DIM, D), (H * MLA_V_DIM) ** -0.5),
        "gm_w_in": nrm((N_GM, D, 2 * GM_WIDTH), D ** -0.5),
        "gm_b_in": nrm((N_GM, 2 * GM_WIDTH), 0.02),
        "gm_v_norm_g": gain((N_GM, GM_WIDTH)),
        "gm_v_norm_b": nrm((N_GM, GM_WIDTH), 0.02),
        "gm_w_s": nrm((N_GM, GM_GROUPS, GM_CHUNK, GM_CHUNK), GM_CHUNK ** -0.5),
        "gm_b_s": nrm((N_GM, GM_GROUPS, GM_CHUNK), 0.02),
        "gm_w_out": nrm((N_GM, GM_WIDTH, D), GM_WIDTH ** -0.5),
        "fn_w_f": nrm((N_FN, D, D), D ** -0.5),
        "fn_b_f": nrm((N_FN, D), 0.02),
    }


def reference(x, c, ctx, c_ctx, norm1_g, norm2_g, ada_a, ada_b, ada_bias,
              ffn_w_gate, ffn_w_val, ffn_conv_w, ffn_conv_b, ffn_w_down,
              mla_w_dq, mla_q_norm_g, mla_w_uq, mla_w_dkv, mla_kv_norm_g, mla_w_ukv,
              mla_qn_nope_g, mla_qn_rope_g, mla_kn_nope_g, mla_kn_rope_g, mla_w_o,
              gm_w_in, gm_b_in, gm_v_norm_g, gm_v_norm_b, gm_w_s, gm_b_s, gm_w_out,
              fn_w_f, fn_b_f):
    ang = axial_angles(x.shape[1])
    xl, xc = x, ctx
    for i in range(DEPTH):
        last = i == DEPTH - 1
        kind, j = i % N_MIXERS, i // N_MIXERS
        sh1, sc1, g1, sh2, sc2, g2 = jnp.split(
            ada_mod(c, ada_a[i], ada_b[i], ada_bias[i])[:, None, :], N_MOD, axis=-1)
        csh1, csc1, cg1, csh2, csc2, cg2 = jnp.split(
            ada_mod(c_ctx, ada_a[i], ada_b[i], ada_bias[i]), N_MOD, axis=-1)
        hl = modulate(xl, norm1_g[i], sh1, sc1)
        if kind == 0:
            hc = modulate(xc, norm1_g[i], csh1, csc1)
            yl, yc = mla_mixer(hl, hc, ang, mla_w_dq[j], mla_q_norm_g[j], mla_w_uq[j],
                               mla_w_dkv[j], mla_kv_norm_g[j], mla_w_ukv[j],
                               mla_qn_nope_g[j], mla_qn_rope_g[j], mla_kn_nope_g[j],
                               mla_kn_rope_g[j], mla_w_o[j], not last)
        elif kind == 1:
            gm = (gm_w_in[j], gm_b_in[j], gm_v_norm_g[j], gm_v_norm_b[j], gm_w_s[j], gm_b_s[j], gm_w_out[j])
            yl = chunk_gmlp(hl, *gm)
            yc = None if last else chunk_gmlp(modulate(xc, norm1_g[i], csh1, csc1), *gm)
        else:
            yl = fourier_mix(hl, fn_w_f[j], fn_b_f[j])
            yc = None if last else fourier_mix(modulate(xc, norm1_g[i], csh1, csc1), fn_w_f[j], fn_b_f[j])
        ffn = (ffn_w_gate[i], ffn_w_val[i], ffn_conv_w[i], ffn_conv_b[i], ffn_w_down[i])
        xl = xl + g1 * yl
        xl = xl + g2 * conv_ffn(modulate(xl, norm2_g[i], sh2, sc2), *ffn)
        if not last:
            xc = xc + cg1 * yc
            xc = xc + cg2 * conv_ffn(modulate(xc, norm2_g[i], csh2, csc2), *ffn)
    return xl
```

```python
import functools

import numpy as np
import jax
import jax.numpy as jnp
from jax import lax
from jax.experimental import pallas as pl
from jax.experimental.pallas import tpu as pltpu

F32 = jnp.float32
BF16 = jnp.bfloat16

NORM_EPS = 1e-6
GRID_W = 64
N_MOD = 6
NOPE_DIM = 128
ROPE_DIM = 64
V_DIM = 128
ROPE_FREQS = ROPE_DIM // 4
ROPE_THETA = 10000.0
FN_GROUPS = 8
N_MIXERS = 3

LANES = 128
HEAD_PAD = 2 * LANES
MOD_ROWS = 8
HALO_ROWS = 16
VMEM_LIMIT_BYTES = 52 * 2 ** 20


def _tile(dim, pref, align):
    if dim <= pref:
        return dim
    t = (pref // align) * align
    while t >= align:
        if dim % t == 0:
            return t
        t -= align
    return dim


def _params(sem):
    return pltpu.CompilerParams(dimension_semantics=sem, vmem_limit_bytes=VMEM_LIMIT_BYTES)


def _sigmoid(x):
    return 1.0 / (1.0 + jnp.exp(-x))


def _rms(x, denom):
    return x * lax.rsqrt(jnp.sum(x * x, axis=-1, keepdims=True) * (1.0 / denom) + NORM_EPS)


def _mm(a, w, epi, outs, extras=(), *, tm, tn, tk, name):
    M, K = a.shape
    batched = w.ndim == 3
    G = w.shape[0] if batched else 1
    N = w.shape[-1]
    assert M % tm == 0 and N % tn == 0 and K % tk == 0, (name, a.shape, w.shape, tm, tn, tk)
    nm, nn, nk = M // tm, N // tn, K // tk
    n_ex, n_out = len(extras), len(outs)

    def body(a_ref, w_ref, *refs):
        ex, out = refs[:n_ex], refs[n_ex:n_ex + n_out]
        g, i, j, kk = (pl.program_id(d) for d in range(4))
        part = jnp.dot(a_ref[...], w_ref[...], preferred_element_type=F32)
        if nk == 1:
            epi(part, ex, out, g, i, j)
            return
        acc = refs[-1]

        @pl.when(kk == 0)
        def _():
            acc[...] = part

        @pl.when(kk > 0)
        def _():
            acc[...] += part

        @pl.when(kk == nk - 1)
        def _():
            epi(acc[...], ex, out, g, i, j)

    if batched:
        w_spec = pl.BlockSpec((None, tk, tn), lambda g, i, j, k: (g, k, j))
    else:
        w_spec = pl.BlockSpec((tk, tn), lambda g, i, j, k: (k, j))
    in_specs = [pl.BlockSpec((tm, tk), lambda g, i, j, k: (i, k)), w_spec]
    in_specs += [pl.BlockSpec(blk, lambda g, i, j, k, f=f: f(g, i, j)) for _, blk, f in extras]
    out_specs = [pl.BlockSpec(blk, lambda g, i, j, k, f=f: f(g, i, j)) for _, _, blk, f in outs]
    res = pl.pallas_call(
        body,
        grid=(G, nm, nn, nk),
        in_specs=in_specs,
        out_specs=out_specs,
        out_shape=[jax.ShapeDtypeStruct(s, d) for s, d, _, _ in outs],
        scratch_shapes=[pltpu.VMEM((tm, tn), F32)] if nk > 1 else [],
        compiler_params=_params(("parallel", "parallel", "parallel", "arbitrary")),
        name=name,
    )(a, w, *[e[0] for e in extras])
    return res


def _out2d(M, N, dtype, tm, tn):
    return ((M, N), dtype, (tm, tn), lambda g, i, j: (i, j))


def _mod_row(i, base, tiles_per_seq):
    return base + i // tiles_per_seq


def _mm_residual(a, w, res, gate, bias, *, base, seq, name, tm=1024, tn=1024, tk=1024):
    M, K = a.shape
    N = w.shape[-1]
    tm = _tile(min(M, seq) if base == 0 else M, tm, 16)
    tn, tk = _tile(N, tn, LANES), _tile(K, tk, LANES)
    tps = max(seq // tm, 1) if base == 0 else M // tm
    has_bias = bias is not None

    def epi(acc, ex, out, g, i, j):
        row = _mod_row(i, base, tps)
        y = acc + ex[2][...] if has_bias else acc
        out[0][...] = ex[0][...] + ex[1][pl.ds(row, 1), :] * y

    extras = [(res, (tm, tn), lambda g, i, j: (i, j)),
              (gate, (MOD_ROWS, tn), lambda g, i, j: (0, j))]
    if has_bias:
        extras.append((bias.reshape(1, N), (1, tn), lambda g, i, j: (0, j)))
    return _mm(a, w, epi, [_out2d(M, N, F32, tm, tn)], extras, tm=tm, tn=tn, tk=tk, name=name)[0]


def _ada_table(cc, ada_a, ada_b, ada_bias):
    L, D, R = ada_a.shape
    NM = ada_b.shape[-1]
    tn = _tile(NM, 4096, LANES)

    def body(c_ref, a_ref, b_ref, bias_ref, o_ref):
        c = c_ref[...]
        s = (c * _sigmoid(c)).astype(BF16)
        t = jnp.dot(s, a_ref[...].astype(BF16), preferred_element_type=F32)
        o = jnp.dot(t.astype(BF16), b_ref[...].astype(BF16), preferred_element_type=F32)
        o_ref[...] = o + bias_ref[...]

    return pl.pallas_call(
        body,
        grid=(L, NM // tn),
        in_specs=[pl.BlockSpec((MOD_ROWS, D), lambda l, j: (0, 0)),
                  pl.BlockSpec((None, D, R), lambda l, j: (l, 0, 0)),
                  pl.BlockSpec((None, R, tn), lambda l, j: (l, 0, j)),
                  pl.BlockSpec((None, 1, tn), lambda l, j: (l, 0, j))],
        out_specs=pl.BlockSpec((None, MOD_ROWS, tn), lambda l, j: (l, 0, j)),
        out_shape=jax.ShapeDtypeStruct((L, MOD_ROWS, NM), F32),
        compiler_params=_params(("parallel", "parallel")),
        name="ada_table",
    )(cc, ada_a, ada_b, ada_bias.reshape(L, 1, NM))


def _modulate(x, g, shift, scale, *, base, seq, name):
    M, D = x.shape
    tm = _tile(min(M, seq), 256, 16)
    tps = max(seq // tm, 1) if base == 0 else M // tm

    def body(x_ref, g_ref, sh_ref, sc_ref, o_ref):
        row = _mod_row(pl.program_id(0), base, tps)
        y = _rms(x_ref[...], D) * g_ref[...]
        o_ref[...] = (y * (1.0 + sc_ref[pl.ds(row, 1), :]) + sh_ref[pl.ds(row, 1), :]).astype(BF16)

    return pl.pallas_call(
        body,
        grid=(M // tm,),
        in_specs=[pl.BlockSpec((tm, D), lambda i: (i, 0)),
                  pl.BlockSpec((1, D), lambda i: (0, 0)),
                  pl.BlockSpec((MOD_ROWS, D), lambda i: (0, 0)),
                  pl.BlockSpec((MOD_ROWS, D), lambda i: (0, 0))],
        out_specs=pl.BlockSpec((tm, D), lambda i: (i, 0)),
        out_shape=jax.ShapeDtypeStruct((M, D), BF16),
        compiler_params=_params(("parallel",)),
        name=name,
    )(x, g.reshape(1, D), shift, scale)


def _ffn_up(h, wg, wv, cw, ghalo, *, tm, tf, tk, seq, name):
    M, K = h.shape
    Fp = wg.shape[1]
    nm, nf, nk = M // tm, Fp // tf, K // tk
    cf = _tile(tf, 256, LANES)

    def finish(i, g_ref, v_ref, cw_ref, halo_ref, u_ref):
        row = lax.broadcasted_iota(jnp.int32, (tm, 1), 0)
        pos = lax.rem(i * tm + row, seq)
        first, last = row == 0, row == tm - 1
        seq_first, seq_last = pos == 0, pos == seq - 1
        for c in range(tf // cf):
            cs = slice(c * cf, (c + 1) * cf)
            g = g_ref[:, cs]
            taps = cw_ref[:, cs]
            halo = halo_ref[:, cs]
            gp = jnp.where(first, halo[0:1, :], pltpu.roll(g, 1, 0))
            gp = jnp.where(seq_first, 0.0, gp)
            gn = jnp.where(last, halo[1:2, :], pltpu.roll(g, tm - 1, 0))
            gn = jnp.where(seq_last, 0.0, gn)
            conv = gp * taps[0:1, :] + g * taps[1:2, :] + gn * taps[2:3, :] + taps[3:4, :]
            u_ref[:, cs] = (conv * _sigmoid(conv) * v_ref[:, cs]).astype(BF16)

    def body(h_ref, wg_ref, wv_ref, cw_ref, halo_ref, u_ref, accg, accv):
        i, kk = pl.program_id(0), pl.program_id(2)
        hb = h_ref[...]
        pg = jnp.dot(hb, wg_ref[...], preferred_element_type=F32)
        pv = jnp.dot(hb, wv_ref[...], preferred_element_type=F32)

        @pl.when(kk == 0)
        def _():
            accg[...] = pg
            accv[...] = pv

        @pl.when(kk > 0)
        def _():
            accg[...] += pg
            accv[...] += pv

        @pl.when(kk == nk - 1)
        def _():
            finish(i, accg, accv, cw_ref, halo_ref, u_ref)

    return pl.pallas_call(
        body,
        grid=(nm, nf, nk),
        in_specs=[pl.BlockSpec((tm, tk), lambda i, j, k: (i, k)),
                  pl.BlockSpec((tk, tf), lambda i, j, k: (k, j)),
                  pl.BlockSpec((tk, tf), lambda i, j, k: (k, j)),
                  pl.BlockSpec((8, tf), lambda i, j, k: (0, j)),
                  pl.BlockSpec((None, HALO_ROWS, tf), lambda i, j, k: (i, 0, j))],
        out_specs=pl.BlockSpec((tm, tf), lambda i, j, k: (i, j)),
        out_shape=jax.ShapeDtypeStruct((M, Fp), BF16),
        scratch_shapes=[pltpu.VMEM((tm, tf), F32), pltpu.VMEM((tm, tf), F32)],
        compiler_params=_params(("parallel", "parallel", "arbitrary")),
        name=name,
    )(h, wg, wv, cw, ghalo)


def _conv_ffn(h, x_res, gate, w_gate, w_val, conv_w, conv_b, w_down, *, base, seq, name):
    M, D = h.shape
    F = w_gate.shape[1]
    tf = 1024 if F > 1024 else _tile(F, 1024, LANES)
    Fp = -(-F // tf) * tf
    pad = Fp - F
    wg = jnp.pad(w_gate.astype(BF16), ((0, 0), (0, pad)))
    wv = jnp.pad(w_val.astype(BF16), ((0, 0), (0, pad)))
    wd = jnp.pad(w_down.astype(BF16), ((0, pad), (0, 0)))
    cw = jnp.pad(jnp.concatenate([conv_w, conv_b[None, :]], axis=0), ((0, 4), (0, pad)))
    tm = _tile(M, 1024, 16)
    tk = _tile(D, 512, LANES)
    nm = M // tm
    idx = np.zeros((nm, HALO_ROWS), np.int32)
    idx[:, 0] = np.maximum(np.arange(nm) * tm - 1, 0)
    idx[:, 1] = np.minimum((np.arange(nm) + 1) * tm, M - 1)
    h_halo = jnp.take(h, jnp.asarray(idx.reshape(-1)), axis=0)

    def epi_plain(acc, ex, out, g, i, j):
        out[0][...] = acc

    mh = nm * HALO_ROWS
    ghalo = _mm(h_halo, wg, epi_plain, [_out2d(mh, Fp, F32, mh, tf)],
                tm=mh, tn=tf, tk=_tile(D, 1024, LANES), name=name + "_halo")[0]
    u = _ffn_up(h, wg, wv, cw, ghalo.reshape(nm, HALO_ROWS, Fp), tm=tm, tf=tf, tk=tk, seq=seq,
                name=name + "_up")
    return _mm_residual(u, wd, x_res, gate, None, base=base, seq=seq, name=name + "_down")


def _rope_tables(n_tok):
    t = jnp.arange(n_tok)
    inv = ROPE_THETA ** (-jnp.arange(ROPE_FREQS, dtype=F32) / ROPE_FREQS)
    row = (t // GRID_W).astype(F32)[:, None] * inv
    col = (t % GRID_W).astype(F32)[:, None] * inv
    z = jnp.zeros((n_tok, LANES - ROPE_DIM), F32)
    cos = jnp.concatenate([jnp.cos(row), jnp.cos(row), jnp.cos(col), jnp.cos(col), z], axis=1)
    sin = jnp.concatenate([-jnp.sin(row), jnp.sin(row), -jnp.sin(col), jnp.sin(col), z], axis=1)
    return cos, sin


def _rope(x, cos, sin):
    lane = lax.broadcasted_iota(jnp.int32, x.shape, 1)
    partner = jnp.where(lane % (2 * ROPE_FREQS) < ROPE_FREQS,
                        pltpu.roll(x, LANES - ROPE_FREQS, 1), pltpu.roll(x, ROPE_FREQS, 1))
    return x * cos + partner * sin


def _pad_lanes(v):
    return jnp.pad(v.astype(F32), (0, LANES - v.shape[0])).reshape(1, LANES)


def _mla_keys(h, p, rope, *, seq, name):
    M, D = h.shape
    kv_rank = p["kv_g"].shape[-1]
    tm = _tile(min(M, seq), 512, 16)
    tps = seq // tm
    use_rope = rope is not None

    def epi_dkv(acc, ex, out, g, i, j):
        out[0][...] = (_rms(acc[:, :kv_rank], kv_rank) * ex[0][...]).astype(BF16)
        r = _rms(acc[:, kv_rank:], ROPE_DIM) * ex[1][...]
        if use_rope:
            r = _rope(r, ex[2][...], ex[3][...])
        out[1][...] = r.astype(BF16)

    extras = [(p["kv_g"], (1, kv_rank), lambda g, i, j: (0, 0)),
              (p["kr_g"], (1, LANES), lambda g, i, j: (0, 0))]
    if use_rope:
        extras += [(t, (tm, LANES), lambda g, i, j: (i % tps, 0)) for t in rope]
    ckv, kr = _mm(h, p["w_dkv"], epi_dkv,
                  [_out2d(M, kv_rank, BF16, tm, kv_rank), _out2d(M, LANES, BF16, tm, LANES)],
                  extras, tm=tm, tn=kv_rank + LANES, tk=_tile(D, 1024, LANES), name=name + "_dkv")

    HN = p["w_uk"].shape[1]
    tn = _tile(HN, 2048, LANES)

    def epi_kn(acc, ex, out, g, i, j):
        for c in range(tn // NOPE_DIM):
            cs = slice(c * NOPE_DIM, (c + 1) * NOPE_DIM)
            out[0][:, cs] = (_rms(acc[:, cs], NOPE_DIM) * ex[0][...]).astype(BF16)

    kn = _mm(ckv, p["w_uk"], epi_kn, [_out2d(M, HN, BF16, tm, tn)],
             [(p["kn_g"], (1, NOPE_DIM), lambda g, i, j: (0, 0))],
             tm=tm, tn=tn, tk=kv_rank, name=name + "_uk")[0]

    def epi_cast(acc, ex, out, g, i, j):
        out[0][...] = acc.astype(BF16)

    v = _mm(ckv, p["w_uv"], epi_cast, [_out2d(M, HN, BF16, tm, tn)],
            tm=tm, tn=tn, tk=kv_rank, name=name + "_uv")[0]
    return kn, kr, v


def _mla_queries(h, p, rope, *, seq, name):
    M, D = h.shape
    q_rank = p["q_g"].shape[-1]
    tm = _tile(min(M, seq), 512, 16)
    tps = seq // tm
    use_rope = rope is not None

    def epi_cq(acc, ex, out, g, i, j):
        out[0][...] = (_rms(acc, q_rank) * ex[0][...]).astype(BF16)

    cq = _mm(h, p["w_dq"], epi_cq, [_out2d(M, q_rank, BF16, tm, q_rank)],
             [(p["q_g"], (1, q_rank), lambda g, i, j: (0, 0))],
             tm=tm, tn=q_rank, tk=_tile(D, 1024, LANES), name=name + "_dq")[0]

    HQ = p["w_uq"].shape[1]
    tn = _tile(HQ, 1024, HEAD_PAD)

    def epi_q(acc, ex, out, g, i, j):
        for c in range(tn // HEAD_PAD):
            lo = c * HEAD_PAD
            qn = _rms(acc[:, lo:lo + NOPE_DIM], NOPE_DIM) * ex[0][...]
            out[0][:, lo:lo + NOPE_DIM] = qn.astype(BF16)
            qr = _rms(acc[:, lo + NOPE_DIM:lo + HEAD_PAD], ROPE_DIM) * ex[1][...]
            if use_rope:
                qr = _rope(qr, ex[2][...], ex[3][...])
            out[0][:, lo + NOPE_DIM:lo + HEAD_PAD] = qr.astype(BF16)

    extras = [(p["qn_g"], (1, NOPE_DIM), lambda g, i, j: (0, 0)),
              (p["qr_g"], (1, LANES), lambda g, i, j: (0, 0))]
    if use_rope:
        extras += [(t, (tm, LANES), lambda g, i, j: (i % tps, 0)) for t in rope]
    return _mm(cq, p["w_uq"], epi_q, [_out2d(M, HQ, BF16, tm, tn)], extras,
               tm=tm, tn=tn, tk=q_rank, name=name + "_uq")[0]


def _attention(q, segs, *, heads, scale, name):
    B, Nq, _ = q.shape
    lens = [s[0].shape[1] for s in segs]
    Nk = sum(lens)
    tq = _tile(Nq, 256, 16)
    n_seg = len(segs)

    def body(q_ref, *refs):
        seg_refs = refs[:3 * n_seg]
        o_ref, k_buf, v_buf = refs[3 * n_seg:]

        @pl.when(pl.program_id(2) == 0)
        def _():
            off = 0
            for s in range(n_seg):
                kn_ref, kr_ref, v_ref = seg_refs[3 * s:3 * s + 3]
                k_buf[off:off + lens[s], 0:NOPE_DIM] = kn_ref[...]
                k_buf[off:off + lens[s], NOPE_DIM:HEAD_PAD] = kr_ref[...]
                v_buf[off:off + lens[s], :] = v_ref[...]
                off += lens[s]

        s = lax.dot_general(q_ref[...], k_buf[...], (((1,), (1,)), ((), ())),
                            preferred_element_type=F32) * scale
        m = jnp.max(s, axis=-1, keepdims=True)
        e = jnp.exp(s - m)
        l = jnp.sum(e, axis=-1, keepdims=True)
        o = jnp.dot(e.astype(BF16), v_buf[...], preferred_element_type=F32)
        o_ref[...] = (o / l).astype(BF16)

    in_specs = [pl.BlockSpec((None, tq, HEAD_PAD), lambda b, h, i: (b, i, h))]
    args = [q]
    for (kn, kr, v), n in zip(segs, lens):
        in_specs += [pl.BlockSpec((None, n, NOPE_DIM), lambda b, h, i: (b, 0, h)),
                     pl.BlockSpec((None, n, LANES), lambda b, h, i: (b, 0, 0)),
                     pl.BlockSpec((None, n, V_DIM), lambda b, h, i: (b, 0, h))]
        args += [kn, kr, v]
    return pl.pallas_call(
        body,
        grid=(B, heads, Nq // tq),
        in_specs=in_specs,
        out_specs=pl.BlockSpec((None, tq, V_DIM), lambda b, h, i: (b, i, h)),
        out_shape=jax.ShapeDtypeStruct((B, Nq, heads * V_DIM), BF16),
        scratch_shapes=[pltpu.VMEM((Nk, HEAD_PAD), BF16), pltpu.VMEM((Nk, V_DIM), BF16)],
        compiler_params=_params(("parallel", "parallel", "arbitrary")),
        name=name,
    )(*args)


def _mla_weights(w_dq, q_norm_g, w_uq, w_dkv, kv_norm_g, w_ukv, qn_nope_g, qn_rope_g,
                 kn_nope_g, kn_rope_g):
    q_rank = w_dq.shape[1]
    kv_rank = kv_norm_g.shape[0]
    heads = w_uq.shape[1] // (NOPE_DIM + ROPE_DIM)
    w_uq_p = jnp.pad(w_uq.astype(BF16).reshape(q_rank, heads, NOPE_DIM + ROPE_DIM),
                     ((0, 0), (0, 0), (0, HEAD_PAD - NOPE_DIM - ROPE_DIM)))
    w_ukv_h = w_ukv.astype(BF16).reshape(kv_rank, heads, NOPE_DIM + V_DIM)
    return heads, {
        "w_dq": w_dq.astype(BF16),
        "q_g": q_norm_g.reshape(1, q_rank),
        "w_uq": w_uq_p.reshape(q_rank, heads * HEAD_PAD),
        "w_dkv": jnp.pad(w_dkv.astype(BF16), ((0, 0), (0, LANES - ROPE_DIM))),
        "kv_g": kv_norm_g.reshape(1, kv_rank),
        "w_uk": w_ukv_h[:, :, :NOPE_DIM].reshape(kv_rank, heads * NOPE_DIM),
        "w_uv": w_ukv_h[:, :, NOPE_DIM:].reshape(kv_rank, heads * V_DIM),
        "qn_g": qn_nope_g.reshape(1, NOPE_DIM),
        "qr_g": _pad_lanes(qn_rope_g),
        "kn_g": kn_nope_g.reshape(1, NOPE_DIM),
        "kr_g": _pad_lanes(kn_rope_g),
    }


def _mla_mixer(hl, hc, rope, heads, p, B, ctx_out, tag):
    S, C = hl.shape[0] // B, hc.shape[0] // B
    scale = float(NOPE_DIM + ROPE_DIM) ** -0.5
    kn_c, kr_c, v_c = _mla_keys(hc, p, None, seq=C, name=tag + "_kc")
    kn_l, kr_l, v_l = _mla_keys(hl, p, rope, seq=S, name=tag + "_kl")
    q_l = _mla_queries(hl, p, rope, seq=S, name=tag + "_ql")
    b3 = lambda t, n: t.reshape(B, n, t.shape[-1])
    seg_c = (b3(kn_c, C), b3(kr_c, C), b3(v_c, C))
    seg_l = (b3(kn_l, S), b3(kr_l, S), b3(v_l, S))
    ol = _attention(b3(q_l, S), [seg_c, seg_l], heads=heads, scale=scale, name=tag + "_attn_l")
    oc = None
    if ctx_out:
        q_c = _mla_queries(hc, p, None, seq=C, name=tag + "_qc")
        oc = _attention(b3(q_c, C), [seg_c], heads=heads, scale=scale, name=tag + "_attn_c")
        oc = oc.reshape(B * C, heads * V_DIM)
    return ol.reshape(B * S, heads * V_DIM), oc


def _gelu_tanh(x):
    return x * (0.5 * (1.0 + jnp.tanh(0.7978845608028654 * (x + 0.044715 * (x * x * x)))))


def _gm_spatial(v, u, ln_g, ln_b, w_s, b_s_t, *, name):
    M, W = v.shape
    G, C, _ = w_s.shape
    wg = W // G
    tm = _tile(M, 2 * C, C)

    def body(v_ref, u_ref, g_ref, b_ref, ws_ref, bs_ref, o_ref):
        x = v_ref[...]
        d = x - jnp.mean(x, axis=-1, keepdims=True)
        y = d * lax.rsqrt(jnp.mean(d * d, axis=-1, keepdims=True) + NORM_EPS)
        y = (y * g_ref[...] + b_ref[...]).astype(BF16)
        for c in range(tm // C):
            rs = slice(c * C, (c + 1) * C)
            for g in range(G):
                cs = slice(g * wg, (g + 1) * wg)
                sv = jnp.dot(ws_ref[g], y[rs, cs], preferred_element_type=F32) + bs_ref[:, g:g + 1]
                o_ref[rs, cs] = (u_ref[rs, cs].astype(F32) * sv).astype(BF16)

    return pl.pallas_call(
        body,
        grid=(M // tm,),
        in_specs=[pl.BlockSpec((tm, W), lambda i: (i, 0)),
                  pl.BlockSpec((tm, W), lambda i: (i, 0)),
                  pl.BlockSpec((1, W), lambda i: (0, 0)),
                  pl.BlockSpec((1, W), lambda i: (0, 0)),
                  pl.BlockSpec((G, C, C), lambda i: (0, 0, 0)),
                  pl.BlockSpec((C, G), lambda i: (0, 0))],
        out_specs=pl.BlockSpec((tm, W), lambda i: (i, 0)),
        out_shape=jax.ShapeDtypeStruct((M, W), BF16),
        compiler_params=_params(("parallel",)),
        name=name,
    )(v, u, ln_g.reshape(1, W), ln_b.reshape(1, W), w_s, b_s_t)


def _gmlp_mixer(h, w_in, b_in, v_norm_g, v_norm_b, w_s, b_s, tag):
    M, D = h.shape
    W = w_in.shape[1] // 2
    tm, tn, tk = _tile(M, 1024, 16), _tile(W, 1024, LANES), _tile(D, 1024, LANES)
    w_in = w_in.astype(BF16)

    def epi_gelu(dtype):
        def epi(acc, ex, out, g, i, j):
            out[0][...] = _gelu_tanh(acc + ex[0][...]).astype(dtype)
        return epi

    def half(w, b, dtype, name):
        return _mm(h, w, epi_gelu(dtype), [_out2d(M, W, dtype, tm, tn)],
                   [(b.reshape(1, W), (1, tn), lambda g, i, j: (0, j))],
                   tm=tm, tn=tn, tk=tk, name=name)[0]

    u = half(w_in[:, :W], b_in[:W], BF16, tag + "_in_u")
    v = half(w_in[:, W:], b_in[W:], F32, tag + "_in_v")
    return _gm_spatial(v, u, v_norm_g, v_norm_b, w_s.astype(BF16), b_s.T, name=tag + "_spatial")


def _dft_tables(n):
    k = jnp.arange(n, dtype=jnp.int32)
    ang = ((k[:, None] * k[None, :]) % n).astype(F32) * (2.0 * np.pi / n)
    return jnp.cos(ang), jnp.sin(ang)


def _fourier_mixer(h, B, tag):
    M, D = h.shape
    n = M // B
    dg = D // FN_GROUPS
    cd, sd = _dft_tables(dg)
    w1 = jnp.concatenate([cd, sd], axis=1).astype(BF16)
    cn, sn = _dft_tables(n)
    a2 = jnp.concatenate([cn, -sn], axis=1).astype(BF16)
    norm = float(n * dg) ** -0.5

    R = M * FN_GROUPS
    rb = n * FN_GROUPS
    tm1 = _tile(rb, 4096, 16)
    tpb = rb // tm1

    def epi_cast(acc, ex, out, g, i, j):
        out[0][...] = acc.astype(BF16)

    xcs = _mm(h.reshape(R, dg), w1, epi_cast,
              [((B, 2, rb, dg), BF16, (None, None, tm1, dg),
                lambda g, i, j: (i // tpb, j, i % tpb, 0))],
              tm=tm1, tn=dg, tk=dg, name=tag + "_dft_ch")[0]

    def epi_scale(acc, ex, out, g, i, j):
        out[0][...] = (acc * norm).astype(BF16)

    tm2, tn2, tk2 = _tile(n, 1024, 16), _tile(D, 1024, LANES), _tile(2 * n, 1024, LANES)
    f = _mm(a2, xcs.reshape(B, 2 * n, D), epi_scale,
            [((B, n, D), BF16, (None, tm2, tn2), lambda g, i, j: (g, i, j))],
            tm=tm2, tn=tn2, tk=tk2, name=tag + "_dft_seq")[0]
    return f.reshape(M, D)


def kernel(x, c, ctx, c_ctx, norm1_g, norm2_g, ada_a, ada_b, ada_bias, ffn_w_gate, ffn_w_val, ffn_conv_w, ffn_conv_b, ffn_w_down, mla_w_dq, mla_q_norm_g, mla_w_uq, mla_w_dkv, mla_kv_norm_g, mla_w_ukv, mla_qn_nope_g, mla_qn_rope_g, mla_kn_nope_g, mla_kn_rope_g, mla_w_o, gm_w_in, gm_b_in, gm_v_norm_g, gm_v_norm_b, gm_w_s, gm_b_s, gm_w_out, fn_w_f, fn_b_f):
    B, S, D = x.shape
    C = ctx.shape[1]
    depth = norm1_g.shape[0]
    assert B < MOD_ROWS
    ctx_row = B

    cc = jnp.zeros((MOD_ROWS, D), F32).at[:B].set(c).at[ctx_row].set(c_ctx)
    mods = _ada_table(cc, ada_a, ada_b, ada_bias)
    rope = _rope_tables(S)

    xl = x.reshape(B * S, D)
    xc = ctx.reshape(B * C, D)
    lat = dict(base=0, seq=S)
    cx = dict(base=ctx_row, seq=C)

    for i in range(depth):
        last = i == depth - 1
        kind, j = i % N_MIXERS, i // N_MIXERS
        sh1, sc1, g1, sh2, sc2, g2 = (mods[i, :, k * D:(k + 1) * D] for k in range(N_MOD))
        tag = "l%d" % i
        hl = _modulate(xl, norm1_g[i], sh1, sc1, name=tag + "_mod1_l", **lat)
        need_ctx = not last
        hc = None
        if need_ctx or kind == 0:
            hc = _modulate(xc, norm1_g[i], sh1, sc1, name=tag + "_mod1_c", **cx)

        if kind == 0:
            heads, p = _mla_weights(mla_w_dq[j], mla_q_norm_g[j], mla_w_uq[j], mla_w_dkv[j],
                                    mla_kv_norm_g[j], mla_w_ukv[j], mla_qn_nope_g[j],
                                    mla_qn_rope_g[j], mla_kn_nope_g[j], mla_kn_rope_g[j])
            yl, yc = _mla_mixer(hl, hc, rope, heads, p, B, need_ctx, tag + "_mla")
            w_mix, b_mix = mla_w_o[j].astype(BF16), None
        elif kind == 1:
            gm = (gm_w_in[j], gm_b_in[j], gm_v_norm_g[j], gm_v_norm_b[j], gm_w_s[j], gm_b_s[j])
            yl = _gmlp_mixer(hl, *gm, tag + "_gm_l")
            yc = _gmlp_mixer(hc, *gm, tag + "_gm_c") if need_ctx else None
            w_mix, b_mix = gm_w_out[j].astype(BF16), None
        else:
            yl = _fourier_mixer(hl, B, tag + "_fn_l")
            yc = _fourier_mixer(hc, B, tag + "_fn_c") if need_ctx else None
            w_mix, b_mix = fn_w_f[j].astype(BF16), fn_b_f[j]

        ffn = (ffn_w_gate[i], ffn_w_val[i], ffn_conv_w[i], ffn_conv_b[i], ffn_w_down[i])
        xl = _mm_residual(yl, w_mix, xl, g1, b_mix, name=tag + "_mix_l", **lat)
        h2 = _modulate(xl, norm2_g[i], sh2, sc2, name=tag + "_mod2_l", **lat)
        xl = _conv_ffn(h2, xl, g2, *ffn, name=tag + "_ffn_l", **lat)
        if need_ctx:
            xc = _mm_residual(yc, w_mix, xc, g1, b_mix, name=tag + "_mix_c", **cx)
            h2 = _modulate(xc, norm2_g[i], sh2, sc2, name=tag + "_mod2_c", **cx)
            xc = _conv_ffn(h2, xc, g2, *ffn, name=tag + "_ffn_c", **cx)
    return xl.reshape(B, S, D)
```

```python
import math

import numpy as np
import jax
import jax.numpy as jnp
from jax import lax
from jax.experimental import pallas as pl
from jax.experimental.pallas import tpu as pltpu

F32 = jnp.float32
BF16 = jnp.bfloat16

NORM_EPS = 1e-6
GRID_W = 64
N_MOD = 6
NOPE_DIM = 128
ROPE_DIM = 64
V_DIM = 128
ROPE_FREQS = ROPE_DIM // 4
ROPE_THETA = 10000.0
FN_GROUPS = 8
N_MIXERS = 3
LOG2E = math.log2(math.e)

LANES = 128
HEAD_PAD = 2 * LANES
MOD_ROWS = 8
HALO_ROWS = 16
VMEM_LIMIT_BYTES = 56 * 2 ** 20


def _tile(dim, pref, align):
    if dim <= pref:
        return dim
    t = (pref // align) * align
    while t >= align:
        if dim % t == 0:
            return t
        t -= align
    return dim


def _params(sem):
    return pltpu.CompilerParams(dimension_semantics=sem, vmem_limit_bytes=VMEM_LIMIT_BYTES)


def _rms(x, denom):
    return x * lax.rsqrt(jnp.sum(x * x, axis=-1, keepdims=True) * (1.0 / denom) + NORM_EPS)


def _mm(a, w, epi, outs, extras=(), *, tm, tn, tk, name, wsel=None, wcol0=0, n=None, cn=512):
    M, K = a.shape[-2:]
    batched = w.ndim == 3 and wsel is None
    G = w.shape[0] if batched else 1
    N = w.shape[-1] if n is None else n
    assert M % tm == 0 and N % tn == 0 and K % tk == 0, (name, a.shape, w.shape, tm, tn, tk)
    nm, nn, nk = M // tm, N // tn, K // tk
    w_rows = w.shape[-2]
    assert K - tk < w_rows <= K
    n_ex, n_out = len(extras), len(outs)
    cn = tn if cn is None or tn % cn else cn
    chunks = [slice(c * cn, (c + 1) * cn) for c in range(tn // cn)]

    def body(a_ref, w_ref, *refs):
        ex, out = refs[:n_ex], refs[n_ex:n_ex + n_out]
        g, i, j, kk = (pl.program_id(d) for d in range(4))
        acc = refs[-1] if nk > 1 else None

        def k_step(first, last):
            for cols in chunks:
                w = w_ref[:, cols]
                if last and w_rows < K:
                    row = lax.broadcasted_iota(jnp.int32, (tk, 1), 0)
                    w = jnp.where(row < w_rows - (nk - 1) * tk, w, jnp.zeros_like(w))
                part = jnp.dot(a_ref[...], w, preferred_element_type=F32)
                if not first:
                    part = acc[:, cols] + part
                if last:
                    epi(part, ex, out, g, i, j, cols)
                else:
                    acc[:, cols] = part

        if nk == 1:
            k_step(True, True)
            return
        pl.when(kk == 0)(lambda: k_step(True, False))
        if nk > 2:
            pl.when((kk > 0) & (kk < nk - 1))(lambda: k_step(False, False))
        pl.when(kk == nk - 1)(lambda: k_step(False, True))

    if batched:
        w_spec = pl.BlockSpec((None, tk, tn), lambda g, i, j, k: (g, k, j + wcol0))
    elif w.ndim == 3:
        w_spec = pl.BlockSpec((None, tk, tn), lambda g, i, j, k: (wsel, k, j + wcol0))
    else:
        w_spec = pl.BlockSpec((tk, tn), lambda g, i, j, k: (k, j + wcol0))
    if a.ndim == 3:
        ga = a.shape[0]
        a_spec = pl.BlockSpec((None, tm, tk), lambda g, i, j, k: (g % ga, i, k))
    else:
        a_spec = pl.BlockSpec((tm, tk), lambda g, i, j, k: (i, k))
    in_specs = [a_spec, w_spec]
    in_specs += [pl.BlockSpec(blk, lambda g, i, j, k, f=f: f(g, i, j)) for _, blk, f in extras]
    out_specs = [pl.BlockSpec(blk, lambda g, i, j, k, f=f: f(g, i, j)) for _, _, blk, f in outs]
    return pl.pallas_call(
        body,
        grid=(G, nm, nn, nk),
        in_specs=in_specs,
        out_specs=out_specs,
        out_shape=[jax.ShapeDtypeStruct(s, d) for s, d, _, _ in outs],
        scratch_shapes=[pltpu.VMEM((tm, tn), F32)] if nk > 1 else [],
        compiler_params=_params(("parallel", "parallel", "parallel", "arbitrary")),
        name=name,
    )(a, w, *[e[0] for e in extras])


def _out2d(M, N, dtype, tm, tn):
    return ((M, N), dtype, (tm, tn), lambda g, i, j: (i, j))


def _mod_row(i, base, tiles_per_seq):
    return base + i // tiles_per_seq


def _mm_residual(a, w, wsel, res, gate, bias, *, base, seq, name, tm=1024, tn=1024, tk=4096):
    M, K = a.shape
    N = w.shape[-1]
    tm = _tile(min(M, seq) if base == 0 else M, tm, 16)
    tn, tk = _tile(N, tn, LANES), _tile(K, tk, LANES)
    tps = max(seq // tm, 1) if base == 0 else M // tm
    has_bias = bias is not None

    def epi(acc, ex, out, g, i, j, cols):
        row = _mod_row(i, base, tps)
        y = acc + ex[2][:, cols] if has_bias else acc
        out[0][:, cols] = ex[0][:, cols] + ex[1][pl.ds(row, 1), cols] * y

    extras = [(res, (tm, tn), lambda g, i, j: (i, j)),
              (gate, (MOD_ROWS, tn), lambda g, i, j: (0, j))]
    if has_bias:
        extras.append((bias.reshape(1, N), (1, tn), lambda g, i, j: (0, j)))
    return _mm(a, w, epi, [_out2d(M, N, F32, tm, tn)], extras, tm=tm, tn=tn, tk=tk,
               wsel=wsel, name=name)[0]


def _ada_table(cc, ada_a, ada_b, ada_bias):
    L, D, R = ada_a.shape
    NM = ada_b.shape[-1]
    tn = _tile(NM, 4096, LANES)

    def body(c_ref, a_ref, b_ref, bias_ref, o_ref):
        c = c_ref[...]
        s = (c * (1.0 / (1.0 + jnp.exp(-c)))).astype(BF16)
        t = jnp.dot(s, a_ref[...].astype(BF16), preferred_element_type=F32)
        o = jnp.dot(t.astype(BF16), b_ref[...].astype(BF16), preferred_element_type=F32)
        o_ref[...] = o + bias_ref[...]

    return pl.pallas_call(
        body,
        grid=(L, NM // tn),
        in_specs=[pl.BlockSpec((MOD_ROWS, D), lambda l, j: (0, 0)),
                  pl.BlockSpec((None, D, R), lambda l, j: (l, 0, 0)),
                  pl.BlockSpec((None, R, tn), lambda l, j: (l, 0, j)),
                  pl.BlockSpec((None, 1, tn), lambda l, j: (l, 0, j))],
        out_specs=pl.BlockSpec((None, MOD_ROWS, tn), lambda l, j: (l, 0, j)),
        out_shape=jax.ShapeDtypeStruct((L, MOD_ROWS, NM), F32),
        compiler_params=_params(("parallel", "parallel")),
        name="ada_table",
    )(cc, ada_a, ada_b, ada_bias.reshape(L, 1, NM))


def _modulate(x, g, shift, scale, *, base, seq, name):
    M, D = x.shape
    tm = _tile(min(M, seq), 512, 16)
    tps = max(seq // tm, 1) if base == 0 else M // tm

    def body(x_ref, g_ref, sh_ref, sc_ref, o_ref):
        row = _mod_row(pl.program_id(0), base, tps)
        y = _rms(x_ref[...], D) * g_ref[...]
        o_ref[...] = (y * (1.0 + sc_ref[pl.ds(row, 1), :]) + sh_ref[pl.ds(row, 1), :]).astype(BF16)

    return pl.pallas_call(
        body,
        grid=(M // tm,),
        in_specs=[pl.BlockSpec((tm, D), lambda i: (i, 0)),
                  pl.BlockSpec((1, D), lambda i: (0, 0)),
                  pl.BlockSpec((MOD_ROWS, D), lambda i: (0, 0)),
                  pl.BlockSpec((MOD_ROWS, D), lambda i: (0, 0))],
        out_specs=pl.BlockSpec((tm, D), lambda i: (i, 0)),
        out_shape=jax.ShapeDtypeStruct((M, D), BF16),
        compiler_params=_params(("parallel",)),
        name=name,
    )(x, g.reshape(1, D), shift, scale)


def _ffn_up(h, wg, wv, cw, ghalo, layer, *, tm, tf, Fp, seq, name):
    M, K = h.shape
    F = wg.shape[-1]
    nm, nf = M // tm, Fp // tf
    cf = _tile(tf, 256, LANES)
    inner = tm > seq
    assert seq % tm == 0 or tm % seq == 0
    last_valid = F - (nf - 1) * tf
    assert 0 < last_valid <= tf

    eg = 16

    def body(h_ref, wg_ref, wv_ref, cw_ref, halo_ref, u_ref):
        def tile_body(valid):
            hb = h_ref[...]
            if inner:
                pos = lax.rem(lax.broadcasted_iota(jnp.int32, (tm, 1), 0), seq)
                seq_first, seq_last = pos == 0, pos == seq - 1
            edge_row = lax.broadcasted_iota(jnp.int32, (eg, 1), 0)
            for c in range(tf // cf):
                cs = slice(c * cf, (c + 1) * cf)
                if c * cf >= valid:
                    u_ref[:, cs] = jnp.zeros((tm, cf), BF16)
                    continue
                taps = cw_ref[:, cs]
                col_ok = None
                if (c + 1) * cf > valid:
                    col_ok = lax.broadcasted_iota(jnp.int32, (1, cf), 1) < valid - c * cf

                def gated(gp, g, gn, v):
                    conv = gp * taps[0:1, :] + g * taps[1:2, :] + gn * taps[2:3, :] + taps[3:4, :]
                    u = conv * (1.0 / (1.0 + jnp.exp2(conv * -LOG2E))) * v
                    if col_ok is not None:
                        u = jnp.where(col_ok, u, 0.0)
                    return u.astype(BF16)

                g = jnp.dot(hb, wg_ref[:, cs], preferred_element_type=F32)
                v = jnp.dot(hb, wv_ref[:, cs], preferred_element_type=F32)
                gp = pltpu.roll(g, 1, 0)
                gn = pltpu.roll(g, tm - 1, 0)
                if inner:
                    gp = jnp.where(seq_first, 0.0, gp)
                    gn = jnp.where(seq_last, 0.0, gn)
                u_ref[:, cs] = gated(gp, g, gn, v)
                top, bot = slice(0, eg), slice(tm - eg, tm)
                gp_top = jnp.where(edge_row == 0, halo_ref[0:1, cs], gp[top])
                u_ref[top, cs] = gated(gp_top, g[top], gn[top], v[top])
                gn_bot = jnp.where(edge_row == eg - 1, halo_ref[1:2, cs], gn[bot])
                u_ref[bot, cs] = gated(gp[bot], g[bot], gn_bot, v[bot])

        if last_valid == tf:
            tile_body(tf)
        else:
            j = pl.program_id(1)
            pl.when(j < nf - 1)(lambda: tile_body(tf))
            pl.when(j == nf - 1)(lambda: tile_body(last_valid))

    return pl.pallas_call(
        body,
        grid=(nm, nf),
        in_specs=[pl.BlockSpec((tm, K), lambda i, j: (i, 0)),
                  pl.BlockSpec((None, K, tf), lambda i, j: (layer, 0, j)),
                  pl.BlockSpec((None, K, tf), lambda i, j: (layer, 0, j)),
                  pl.BlockSpec((None, 8, tf), lambda i, j: (layer, 0, j)),
                  pl.BlockSpec((None, HALO_ROWS, tf), lambda i, j: (i, 0, j))],
        out_specs=pl.BlockSpec((tm, tf), lambda i, j: (i, j)),
        out_shape=jax.ShapeDtypeStruct((M, Fp), BF16),
        compiler_params=_params(("parallel", "parallel")),
        name=name,
    )(h, wg, wv, cw, ghalo)


def _ffn_weights(w_gate, w_val, conv_w, conv_b, w_down):
    F = w_gate.shape[-1]
    tf = 512 if F > 512 else F
    Fp = -(-F // tf) * tf
    cw = jnp.pad(jnp.concatenate([conv_w, conv_b[:, None, :]], axis=1),
                 ((0, 0), (0, 4), (0, Fp - F)))
    return dict(wg=w_gate.astype(BF16), wv=w_val.astype(BF16), wd=w_down.astype(BF16), cw=cw,
                tf=tf, Fp=Fp)


def _conv_ffn(h, x_res, gate, fw, layer, *, base, seq, name):
    M, D = h.shape
    Fp, tf = fw["Fp"], fw["tf"]
    tm = _tile(M, 1024, 16)
    nm = M // tm
    idx = np.zeros((nm, HALO_ROWS), np.int32)
    ok = np.zeros((nm, HALO_ROWS), bool)
    starts = np.arange(nm) * tm
    idx[:, 0], ok[:, 0] = np.maximum(starts - 1, 0), starts % seq != 0
    idx[:, 1], ok[:, 1] = np.minimum(starts + tm, M - 1), (starts + tm) % seq != 0
    h_halo = jnp.where(jnp.asarray(ok.reshape(-1, 1)),
                       jnp.take(h, jnp.asarray(idx.reshape(-1)), axis=0), 0).astype(BF16)

    def epi_plain(acc, ex, out, g, i, j, cols):
        out[0][:, cols] = acc

    mh = nm * HALO_ROWS
    ghalo = _mm(h_halo, fw["wg"], epi_plain, [_out2d(mh, Fp, F32, mh, tf)],
                tm=mh, tn=tf, tk=D, wsel=layer, n=Fp, name=name + "_halo")[0]
    u = _ffn_up(h, fw["wg"], fw["wv"], fw["cw"], ghalo.reshape(nm, HALO_ROWS, Fp), layer,
                tm=tm, tf=tf, Fp=Fp, seq=seq, name=name + "_up")
    return _mm_residual(u, fw["wd"], layer, x_res, gate, None, base=base, seq=seq,
                        tk=Fp // 4 if Fp % (4 * LANES) == 0 else Fp, name=name + "_down")


def _rope_tables(n_tok):
    t = jnp.arange(n_tok)
    inv = ROPE_THETA ** (-jnp.arange(ROPE_FREQS, dtype=F32) / ROPE_FREQS)
    row = (t // GRID_W).astype(F32)[:, None] * inv
    col = (t % GRID_W).astype(F32)[:, None] * inv
    z = jnp.zeros((n_tok, LANES - ROPE_DIM), F32)
    cos = jnp.concatenate([jnp.cos(row), jnp.cos(row), jnp.cos(col), jnp.cos(col), z], axis=1)
    sin = jnp.concatenate([-jnp.sin(row), jnp.sin(row), -jnp.sin(col), jnp.sin(col), z], axis=1)
    return cos, sin


ROPE_PARTNER = np.array([l + ROPE_FREQS if l % (2 * ROPE_FREQS) < ROPE_FREQS else l - ROPE_FREQS
                         for l in range(ROPE_DIM)])


def _rope(x, cos, sin):
    return x * cos + pltpu.roll(x, ROPE_DIM, 1) * sin


def _rope_cols(w):
    return jnp.concatenate([w, w[..., ROPE_PARTNER]], axis=-1)


def _rope_gains(g, scale=1.0):
    g = g.astype(F32) * scale
    return (_rope_cols(g)[:, None, :],
            jnp.concatenate([g, jnp.zeros_like(g)], axis=-1)[:, None, :])


def _head_major(M, seq, heads, width, tm, tn):
    tps = seq // tm
    return ((M // seq, heads, seq, width), BF16, (None, tn // width, tm, width),
            lambda g, i, jj: (i // tps, jj, i % tps, 0))


def _mla_keys(h, p, j, rope, *, seq, name):
    M, D = h.shape
    kv_rank = p["kv_g"].shape[-1]
    tm = _tile(min(M, seq), 1024, 16)
    tps = seq // tm
    use_rope = rope is not None

    def epi_dkv(acc, ex, out, g, i, jj, cols):
        out[0][...] = (_rms(acc[:, :kv_rank], kv_rank) * ex[0][...]).astype(BF16)
        r = _rms(acc[:, kv_rank:], 2 * ROPE_DIM) * ex[1][...]
        if use_rope:
            r = _rope(r, ex[2][...], ex[3][...])
        out[1][...] = r.astype(BF16)

    extras = [(p["kv_g"], (None, 1, kv_rank), lambda g, i, jj: (j, 0, 0)),
              (p["kr_g"][0 if use_rope else 1], (None, 1, LANES), lambda g, i, jj: (j, 0, 0))]
    if use_rope:
        extras += [(t, (tm, LANES), lambda g, i, jj: (i % tps, 0)) for t in rope]
    ckv, kr = _mm(h, p["w_dkv"], epi_dkv,
                  [_out2d(M, kv_rank, BF16, tm, kv_rank), _out2d(M, LANES, BF16, tm, LANES)],
                  extras, tm=tm, tn=kv_rank + LANES, tk=D, wsel=j, cn=None, name=name + "_dkv")

    HN = p["w_uk"].shape[-1]
    heads = HN // NOPE_DIM
    tn = _tile(HN, 2048, LANES)

    def epi_kn(acc, ex, out, g, i, jj, cols):
        for c in range(acc.shape[1] // NOPE_DIM):
            cs = slice(c * NOPE_DIM, (c + 1) * NOPE_DIM)
            out[0][cols.start // NOPE_DIM + c] = (
                _rms(acc[:, cs], NOPE_DIM) * ex[0][...]).astype(BF16)

    kn = _mm(ckv, p["w_uk"], epi_kn, [_head_major(M, seq, heads, NOPE_DIM, tm, tn)],
             [(p["kn_g"], (None, 1, NOPE_DIM), lambda g, i, jj: (j, 0, 0))],
             tm=tm, tn=tn, tk=kv_rank, wsel=j, name=name + "_uk")[0]

    def epi_cast(acc, ex, out, g, i, jj, cols):
        for c in range(acc.shape[1] // V_DIM):
            out[0][cols.start // V_DIM + c] = acc[:, c * V_DIM:(c + 1) * V_DIM].astype(BF16)

    v = _mm(ckv, p["w_uv"], epi_cast, [_head_major(M, seq, heads, V_DIM, tm, tn)],
            tm=tm, tn=tn, tk=kv_rank, wsel=j, name=name + "_uv")[0]
    return kn, kr.reshape(M // seq, seq, LANES), v


def _mla_queries(h, p, j, rope, *, seq, name):
    M, D = h.shape
    q_rank = p["q_g"].shape[-1]
    tm = _tile(min(M, seq), 512, 16)
    tps = seq // tm
    use_rope = rope is not None

    def epi_cq(acc, ex, out, g, i, jj, cols):
        out[0][...] = (_rms(acc, q_rank) * ex[0][...]).astype(BF16)

    cq = _mm(h, p["w_dq"], epi_cq, [_out2d(M, q_rank, BF16, tm, q_rank)],
             [(p["q_g"], (None, 1, q_rank), lambda g, i, jj: (j, 0, 0))],
             tm=tm, tn=q_rank, tk=D, wsel=j, cn=None, name=name + "_dq")[0]

    HQ = p["w_uq"].shape[-1]
    tn = _tile(HQ, 2048, HEAD_PAD)
    tm = _tile(min(M, seq), 1024, 16)
    tps = seq // tm

    def epi_q(acc, ex, out, g, i, jj, cols):
        for c in range(acc.shape[1] // HEAD_PAD):
            lo, hd = c * HEAD_PAD, cols.start // HEAD_PAD + c
            qn = _rms(acc[:, lo:lo + NOPE_DIM], NOPE_DIM) * ex[0][...]
            out[0][hd, :, 0:NOPE_DIM] = qn.astype(BF16)
            qr = _rms(acc[:, lo + NOPE_DIM:lo + HEAD_PAD], 2 * ROPE_DIM) * ex[1][...]
            if use_rope:
                qr = _rope(qr, ex[2][...], ex[3][...])
            out[0][hd, :, NOPE_DIM:HEAD_PAD] = qr.astype(BF16)

    extras = [(p["qn_g"], (None, 1, NOPE_DIM), lambda g, i, jj: (j, 0, 0)),
              (p["qr_g"][0 if use_rope else 1], (None, 1, LANES), lambda g, i, jj: (j, 0, 0))]
    if use_rope:
        extras += [(t, (tm, LANES), lambda g, i, jj: (i % tps, 0)) for t in rope]
    return _mm(cq, p["w_uq"], epi_q, [_head_major(M, seq, HQ // HEAD_PAD, HEAD_PAD, tm, tn)],
               extras, tm=tm, tn=tn, tk=q_rank, wsel=j, cn=HEAD_PAD, name=name + "_uq")[0]


def _attention(q, segs, *, heads, name):
    B, _, Nq, _ = q.shape
    lens = [s[1].shape[1] for s in segs]
    Nk = sum(lens)
    tq = _tile(Nq, 1024, 16)
    ck = min(512, Nk)
    chunks = [(o, min(ck, Nk - o)) for o in range(0, Nk, ck)]
    n_seg = len(segs)

    hq = tq // 2

    def body(q_ref, *refs):
        seg_refs = refs[:3 * n_seg]
        o_ref, k_buf, v_buf, s_buf = refs[3 * n_seg:]

        lane = lax.broadcasted_iota(jnp.int32, (Nk, LANES), 1)
        v_buf[:, V_DIM:] = jnp.where(lane == 0, 1.0, 0.0).astype(BF16)
        off = 0
        for s in range(n_seg):
            kn_ref, kr_ref, v_ref = seg_refs[3 * s:3 * s + 3]
            k_buf[off:off + lens[s], 0:NOPE_DIM] = kn_ref[...]
            k_buf[off:off + lens[s], NOPE_DIM:HEAD_PAD] = kr_ref[...]
            v_buf[off:off + lens[s], 0:V_DIM] = v_ref[...]
            off += lens[s]

        def q_block(qi, carry):
            q0 = pl.multiple_of(qi * tq, tq)
            mx = [None, None]
            for off, n in chunks:
                for hh in range(2):
                    s = lax.dot_general(q_ref[pl.ds(q0 + hh * hq, hq), :], k_buf[off:off + n, :],
                                        (((1,), (1,)), ((), ())), preferred_element_type=F32)
                    s_buf[hh * hq:(hh + 1) * hq, off:off + n] = s
                    for t in range(n // LANES):
                        st = s[:, t * LANES:(t + 1) * LANES]
                        mx[hh] = st if mx[hh] is None else jnp.maximum(mx[hh], st)
            for hh in range(2):
                m = jnp.max(mx[hh], axis=-1, keepdims=True)
                p = jnp.exp2(s_buf[hh * hq:(hh + 1) * hq, :] - m).astype(BF16)
                ov = jnp.dot(p, v_buf[...], preferred_element_type=F32)
                o_ref[pl.ds(q0 + hh * hq, hq), :] = (
                    ov[:, :V_DIM] / ov[:, V_DIM:V_DIM + 1]).astype(BF16)
            return carry

        lax.fori_loop(0, Nq // tq, q_block, 0)

    in_specs = [pl.BlockSpec((None, None, Nq, HEAD_PAD), lambda b, h: (b, h, 0, 0))]
    args = [q]
    for (kn, kr, v), n in zip(segs, lens):
        in_specs += [pl.BlockSpec((None, None, n, NOPE_DIM), lambda b, h: (b, h, 0, 0)),
                     pl.BlockSpec((None, n, LANES), lambda b, h: (b, 0, 0)),
                     pl.BlockSpec((None, None, n, V_DIM), lambda b, h: (b, h, 0, 0))]
        args += [kn, kr, v]
    return pl.pallas_call(
        body,
        grid=(B, heads),
        in_specs=in_specs,
        out_specs=pl.BlockSpec((None, Nq, V_DIM), lambda b, h: (b, 0, h)),
        out_shape=jax.ShapeDtypeStruct((B, Nq, heads * V_DIM), BF16),
        scratch_shapes=[pltpu.VMEM((Nk, HEAD_PAD), BF16), pltpu.VMEM((Nk, V_DIM + LANES), BF16),
                        pltpu.VMEM((tq, Nk), F32)],
        compiler_params=_params(("parallel", "parallel")),
        name=name,
    )(*args)


def _mla_weights(w_dq, q_norm_g, w_uq, w_dkv, kv_norm_g, w_ukv, qn_nope_g, qn_rope_g,
                 kn_nope_g, kn_rope_g, w_o):
    L, _, q_rank = w_dq.shape
    kv_rank = kv_norm_g.shape[-1]
    heads = w_uq.shape[-1] // (NOPE_DIM + ROPE_DIM)
    w_uq_h = w_uq.astype(BF16).reshape(L, q_rank, heads, NOPE_DIM + ROPE_DIM)
    w_uq_p = jnp.concatenate([w_uq_h[..., :NOPE_DIM], _rope_cols(w_uq_h[..., NOPE_DIM:])], axis=-1)
    w_dkv = w_dkv.astype(BF16)
    w_ukv_h = w_ukv.astype(BF16).reshape(L, kv_rank, heads, NOPE_DIM + V_DIM)
    qscale = float(NOPE_DIM + ROPE_DIM) ** -0.5 * LOG2E
    return heads, {
        "w_dq": w_dq.astype(BF16),
        "q_g": q_norm_g[:, None, :],
        "w_uq": w_uq_p.reshape(L, q_rank, heads * HEAD_PAD),
        "w_dkv": jnp.concatenate([w_dkv[..., :kv_rank], _rope_cols(w_dkv[..., kv_rank:])], axis=-1),
        "kv_g": kv_norm_g[:, None, :],
        "w_uk": w_ukv_h[..., :NOPE_DIM].reshape(L, kv_rank, heads * NOPE_DIM),
        "w_uv": w_ukv_h[..., NOPE_DIM:].reshape(L, kv_rank, heads * V_DIM),
        "qn_g": qn_nope_g[:, None, :] * qscale,
        "qr_g": _rope_gains(qn_rope_g, qscale),
        "kn_g": kn_nope_g[:, None, :],
        "kr_g": _rope_gains(kn_rope_g),
        "w_o": w_o.astype(BF16),
    }


def _mla_mixer(hl, hc, rope, heads, p, j, B, ctx_out, tag):
    S, C = hl.shape[0] // B, hc.shape[0] // B
    seg_c = _mla_keys(hc, p, j, None, seq=C, name=tag + "_kc")
    seg_l = _mla_keys(hl, p, j, rope, seq=S, name=tag + "_kl")
    q_l = _mla_queries(hl, p, j, rope, seq=S, name=tag + "_ql")
    ol = _attention(q_l, [seg_c, seg_l], heads=heads, name=tag + "_attn_l")
    oc = None
    if ctx_out:
        q_c = _mla_queries(hc, p, j, None, seq=C, name=tag + "_qc")
        oc = _attention(q_c, [seg_c], heads=heads, name=tag + "_attn_c")
        oc = oc.reshape(B * C, heads * V_DIM)
    return ol.reshape(B * S, heads * V_DIM), oc


def _gelu_tanh(x):
    return x * (0.5 * (1.0 + jnp.tanh(0.7978845608028654 * (x + 0.044715 * (x * x * x)))))


def _gm_spatial(v, u, ln_g, ln_b, w_s, b_s_t, *, name):
    M, W = v.shape
    G, C, _ = w_s.shape
    wg = W // G
    tm = _tile(M, 2 * C, C)

    def body(v_ref, u_ref, g_ref, b_ref, ws_ref, bs_ref, o_ref):
        x = v_ref[...]
        d = x - jnp.mean(x, axis=-1, keepdims=True)
        y = d * lax.rsqrt(jnp.mean(d * d, axis=-1, keepdims=True) + NORM_EPS)
        y = (y * g_ref[...] + b_ref[...]).astype(BF16)
        for c in range(tm // C):
            rs = slice(c * C, (c + 1) * C)
            for g in range(G):
                cs = slice(g * wg, (g + 1) * wg)
                sv = jnp.dot(ws_ref[g], y[rs, cs], preferred_element_type=F32) + bs_ref[:, g:g + 1]
                o_ref[rs, cs] = (u_ref[rs, cs].astype(F32) * sv).astype(BF16)

    return pl.pallas_call(
        body,
        grid=(M // tm,),
        in_specs=[pl.BlockSpec((tm, W), lambda i: (i, 0)),
                  pl.BlockSpec((tm, W), lambda i: (i, 0)),
                  pl.BlockSpec((1, W), lambda i: (0, 0)),
                  pl.BlockSpec((1, W), lambda i: (0, 0)),
                  pl.BlockSpec((G, C, C), lambda i: (0, 0, 0)),
                  pl.BlockSpec((C, G), lambda i: (0, 0))],
        out_specs=pl.BlockSpec((tm, W), lambda i: (i, 0)),
        out_shape=jax.ShapeDtypeStruct((M, W), BF16),
        compiler_params=_params(("parallel",)),
        name=name,
    )(v, u, ln_g.reshape(1, W), ln_b.reshape(1, W), w_s, b_s_t)


def _gmlp_mixer(h, w_in, j, b_in, v_norm_g, v_norm_b, w_s, b_s, tag):
    M, D = h.shape
    W = w_in.shape[-1] // 2
    tm, tn = _tile(M, 1024, 16), _tile(W, 1024, LANES)

    def epi_gelu(dtype):
        def epi(acc, ex, out, g, i, jj, cols):
            out[0][:, cols] = _gelu_tanh(acc + ex[0][:, cols]).astype(dtype)
        return epi

    def half(col0, b, dtype, name):
        return _mm(h, w_in, epi_gelu(dtype), [_out2d(M, W, dtype, tm, tn)],
                   [(b.reshape(1, W), (1, tn), lambda g, i, jj: (0, jj))],
                   tm=tm, tn=tn, tk=D, wsel=j, wcol0=col0, n=W, name=name)[0]

    u = half(0, b_in[:W], BF16, tag + "_in_u")
    v = half(W // tn, b_in[W:], F32, tag + "_in_v")
    return _gm_spatial(v, u, v_norm_g, v_norm_b, w_s.astype(BF16), b_s.T, name=tag + "_spatial")


def _dft_tables(n, k=None, ncols=None):
    ncols = n if ncols is None else ncols
    r = 64 if ncols % 64 == 0 and n % 64 == 0 and ncols > 64 else 1
    k = (jnp.arange(n, dtype=jnp.int32) if k is None else k)[:, None]

    def table(j, period):
        ang = ((k * j[None, :]) % period).astype(F32) * (2.0 * np.pi / period)
        return jnp.cos(ang), jnp.sin(ang)

    if r == 1:
        return table(jnp.arange(ncols, dtype=jnp.int32), n)
    cb, sb = table(jnp.arange(r, dtype=jnp.int32), n)
    ca, sa = table(jnp.arange(ncols // r, dtype=jnp.int32), n // r)
    ca, sa, cb, sb = ca[:, :, None], sa[:, :, None], cb[:, None, :], sb[:, None, :]
    shape = (k.shape[0], ncols)
    return (ca * cb - sa * sb).reshape(shape), (sa * cb + ca * sb).reshape(shape)


def _dft_channels(h, w1, B, *, name):
    M, D = h.shape
    n = M // B
    nh = n // 2
    dg = w1.shape[0]
    tm = _tile(nh, 1024, 16)
    tpb = nh // tm

    def body(lo_ref, hi_ref, w_ref, o_ref):
        lo = jnp.dot(lo_ref[...], w_ref[...], preferred_element_type=F32)
        hi = jnp.dot(hi_ref[...], w_ref[...], preferred_element_type=F32)
        for parity, r in enumerate((lo + hi, lo - hi)):
            o_ref[parity, 0] = r[:, :dg].astype(BF16)
            o_ref[parity, 1] = r[:, dg:].astype(BF16)

    return pl.pallas_call(
        body,
        grid=(B * tpb, D // dg),
        in_specs=[pl.BlockSpec((tm, dg), lambda i, g: ((i // tpb) * 2 * tpb + i % tpb, g)),
                  pl.BlockSpec((tm, dg), lambda i, g: ((i // tpb) * 2 * tpb + tpb + i % tpb, g)),
                  pl.BlockSpec((dg, 2 * dg), lambda i, g: (0, 0))],
        out_specs=pl.BlockSpec((None, 2, 2, tm, dg), lambda i, g: (i // tpb, 0, 0, i % tpb, g)),
        out_shape=jax.ShapeDtypeStruct((B, 2, 2, nh, D), BF16),
        compiler_params=_params(("parallel", "parallel")),
        name=name,
    )(h, h, w1)


def _fourier_mixer(h, B, tag):
    M, D = h.shape
    n = M // B
    nh = n // 2
    dg = D // FN_GROUPS
    cd, sd = _dft_tables(dg)
    w1 = jnp.concatenate([cd, sd], axis=1).astype(BF16)
    freq = 2 * jnp.arange(nh, dtype=jnp.int32)
    a2 = []
    for p in range(2):
        cn, sn = _dft_tables(n, k=freq + p, ncols=nh)
        a2.append(jnp.concatenate([cn, -sn], axis=1))
    a2 = jnp.stack(a2).astype(BF16)
    norm = float(n * dg) ** -0.5
    xcs = _dft_channels(h, w1, B, name=tag + "_dft_ch")

    def epi_scale(acc, ex, out, g, i, j, cols):
        out[0][:, cols] = (acc * norm).astype(BF16)

    tm2, tn2, tk2 = _tile(nh, 1024, 16), _tile(D, 1024, LANES), _tile(n, 2048, LANES)
    f = _mm(a2, xcs.reshape(B * 2, n, D), epi_scale,
            [((B, nh, 2 * D), BF16, (None, tm2, tn2),
              lambda g, i, j: (g // 2, i, (g % 2) * (D // tn2) + j))],
            tm=tm2, tn=tn2, tk=tk2, name=tag + "_dft_seq")[0]
    return f.reshape(M, D)


def kernel(x, c, ctx, c_ctx, norm1_g, norm2_g, ada_a, ada_b, ada_bias, ffn_w_gate, ffn_w_val, ffn_conv_w, ffn_conv_b, ffn_w_down, mla_w_dq, mla_q_norm_g, mla_w_uq, mla_w_dkv, mla_kv_norm_g, mla_w_ukv, mla_qn_nope_g, mla_qn_rope_g, mla_kn_nope_g, mla_kn_rope_g, mla_w_o, gm_w_in, gm_b_in, gm_v_norm_g, gm_v_norm_b, gm_w_s, gm_b_s, gm_w_out, fn_w_f, fn_b_f):
    B, S, D = x.shape
    C = ctx.shape[1]
    depth = norm1_g.shape[0]
    assert B < MOD_ROWS
    ctx_row = B

    cc = jnp.zeros((MOD_ROWS, D), F32).at[:B].set(c).at[ctx_row].set(c_ctx)
    mods = _ada_table(cc, ada_a, ada_b, ada_bias)
    rope = _rope_tables(S)
    fw = _ffn_weights(ffn_w_gate, ffn_w_val, ffn_conv_w, ffn_conv_b, ffn_w_down)
    heads, mp = _mla_weights(mla_w_dq, mla_q_norm_g, mla_w_uq, mla_w_dkv, mla_kv_norm_g,
                             mla_w_ukv, mla_qn_nope_g, mla_qn_rope_g, mla_kn_nope_g,
                             mla_kn_rope_g, mla_w_o)
    gm_in, gm_out, fn_w = gm_w_in.astype(BF16), gm_w_out.astype(BF16), fn_w_f.astype(BF16)

    xl = x.reshape(B * S, D)
    xc = ctx.reshape(B * C, D)
    lat = dict(base=0, seq=S)
    cx = dict(base=ctx_row, seq=C)

    for i in range(depth):
        last = i == depth - 1
        kind, j = i % N_MIXERS, i // N_MIXERS
        sh1, sc1, g1, sh2, sc2, g2 = (mods[i, :, k * D:(k + 1) * D] for k in range(N_MOD))
        tag = "l%d" % i
        hl = _modulate(xl, norm1_g[i], sh1, sc1, name=tag + "_mod1_l", **lat)
        need_ctx = not last
        hc = None
        if need_ctx or kind == 0:
            hc = _modulate(xc, norm1_g[i], sh1, sc1, name=tag + "_mod1_c", **cx)

        if kind == 0:
            yl, yc = _mla_mixer(hl, hc, rope, heads, mp, j, B, need_ctx, tag + "_mla")
            w_mix, b_mix, mix_tk = mp["w_o"], None, 2048
        elif kind == 1:
            gm = (gm_in, j, gm_b_in[j], gm_v_norm_g[j], gm_v_norm_b[j], gm_w_s[j], gm_b_s[j])
            yl = _gmlp_mixer(hl, *gm, tag + "_gm_l")
            yc = _gmlp_mixer(hc, *gm, tag + "_gm_c") if need_ctx else None
            w_mix, b_mix, mix_tk = gm_out, None, 4096
        else:
            yl = _fourier_mixer(hl, B, tag + "_fn_l")
            yc = _fourier_mixer(hc, B, tag + "_fn_c") if need_ctx else None
            w_mix, b_mix, mix_tk = fn_w, fn_b_f[j], 4096

        mix = dict(tn=512 if mix_tk == 4096 else 1024, tk=mix_tk)
        xl = _mm_residual(yl, w_mix, j, xl, g1, b_mix, name=tag + "_mix_l", **mix, **lat)
        h2 = _modulate(xl, norm2_g[i], sh2, sc2, name=tag + "_mod2_l", **lat)
        xl = _conv_ffn(h2, xl, g2, fw, i, name=tag + "_ffn_l", **lat)
        if need_ctx:
            xc = _mm_residual(yc, w_mix, j, xc, g1, b_mix, name=tag + "_mix_c", **mix, **cx)
            h2 = _modulate(xc, norm2_g[i], sh2, sc2, name=tag + "_mod2_c", **cx)
            xc = _conv_ffn(h2, xc, g2, fw, i, name=tag + "_ffn_c", **cx)
    return xl.reshape(B, S, D)
```

```python
import math

import numpy as np
import jax
import jax.numpy as jnp
from jax import lax
from jax.experimental import pallas as pl
from jax.experimental.pallas import tpu as pltpu

F32 = jnp.float32
BF16 = jnp.bfloat16

NORM_EPS = 1e-6
GRID_W = 64
N_MOD = 6
NOPE_DIM = 128
ROPE_DIM = 64
V_DIM = 128
ROPE_FREQS = ROPE_DIM // 4
ROPE_THETA = 10000.0
FN_GROUPS = 8
N_MIXERS = 3
LOG2E = math.log2(math.e)

LANES = 128
HEAD_PAD = 2 * LANES
MOD_ROWS = 8
HALO_ROWS = 16
VMEM_LIMIT_BYTES = 56 * 2 ** 20


def _tile(dim, pref, align):
    if dim <= pref:
        return dim
    t = (pref // align) * align
    while t >= align:
        if dim % t == 0:
            return t
        t -= align
    return dim


def _params(sem):
    return pltpu.CompilerParams(dimension_semantics=sem, vmem_limit_bytes=VMEM_LIMIT_BYTES)


def _rms(x, denom):
    return x * lax.rsqrt(jnp.sum(x * x, axis=-1, keepdims=True) * (1.0 / denom) + NORM_EPS)


def _mm(a, w, epi, outs, extras=(), *, tm, tn, tk, name, wsel=None, wcol0=0, n=None, cn=512):
    M, K = a.shape[-2:]
    batched = w.ndim == 3 and wsel is None
    G = w.shape[0] if batched else 1
    N = w.shape[-1] if n is None else n
    assert M % tm == 0 and N % tn == 0 and K % tk == 0, (name, a.shape, w.shape, tm, tn, tk)
    nm, nn, nk = M // tm, N // tn, K // tk
    w_rows = w.shape[-2]
    assert K - tk < w_rows <= K
    n_ex, n_out = len(extras), len(outs)
    cn = tn if cn is None or tn % cn else cn
    chunks = [slice(c * cn, (c + 1) * cn) for c in range(tn // cn)]

    def body(a_ref, w_ref, *refs):
        ex, out = refs[:n_ex], refs[n_ex:n_ex + n_out]
        g, i, j, kk = (pl.program_id(d) for d in range(4))
        acc = refs[-1] if nk > 1 else None

        def k_step(first, last):
            for cols in chunks:
                w = w_ref[:, cols]
                if last and w_rows < K:
                    row = lax.broadcasted_iota(jnp.int32, (tk, 1), 0)
                    w = jnp.where(row < w_rows - (nk - 1) * tk, w, jnp.zeros_like(w))
                part = jnp.dot(a_ref[...], w, preferred_element_type=F32)
                if not first:
                    part = acc[:, cols] + part
                if last:
                    epi(part, ex, out, g, i, j, cols)
                else:
                    acc[:, cols] = part

        if nk == 1:
            k_step(True, True)
            return
        pl.when(kk == 0)(lambda: k_step(True, False))
        if nk > 2:
            pl.when((kk > 0) & (kk < nk - 1))(lambda: k_step(False, False))
        pl.when(kk == nk - 1)(lambda: k_step(False, True))

    if batched:
        w_spec = pl.BlockSpec((None, tk, tn), lambda g, i, j, k: (g, k, j + wcol0))
    elif w.ndim == 3:
        w_spec = pl.BlockSpec((None, tk, tn), lambda g, i, j, k: (wsel, k, j + wcol0))
    else:
        w_spec = pl.BlockSpec((tk, tn), lambda g, i, j, k: (k, j + wcol0))
    if a.ndim == 3:
        ga = a.shape[0]
        a_spec = pl.BlockSpec((None, tm, tk), lambda g, i, j, k: (g % ga, i, k))
    else:
        a_spec = pl.BlockSpec((tm, tk), lambda g, i, j, k: (i, k))
    in_specs = [a_spec, w_spec]
    in_specs += [pl.BlockSpec(blk, lambda g, i, j, k, f=f: f(g, i, j)) for _, blk, f in extras]
    out_specs = [pl.BlockSpec(blk, lambda g, i, j, k, f=f: f(g, i, j)) for _, _, blk, f in outs]
    return pl.pallas_call(
        body,
        grid=(G, nm, nn, nk),
        in_specs=in_specs,
        out_specs=out_specs,
        out_shape=[jax.ShapeDtypeStruct(s, d) for s, d, _, _ in outs],
        scratch_shapes=[pltpu.VMEM((tm, tn), F32)] if nk > 1 else [],
        compiler_params=_params(("parallel", "parallel", "parallel", "arbitrary")),
        name=name,
    )(a, w, *[e[0] for e in extras])


def _out2d(M, N, dtype, tm, tn):
    return ((M, N), dtype, (tm, tn), lambda g, i, j: (i, j))


def _mod_row(i, base, tiles_per_seq):
    return base + i // tiles_per_seq


def _mm_residual(a, w, wsel, res, gate, bias, *, base, seq, name, tm=1024, tn=1024, tk=4096):
    M, K = a.shape
    N = w.shape[-1]
    tm = _tile(min(M, seq) if base == 0 else M, tm, 16)
    tn, tk = _tile(N, tn, LANES), _tile(K, tk, LANES)
    tps = max(seq // tm, 1) if base == 0 else M // tm
    has_bias = bias is not None

    def epi(acc, ex, out, g, i, j, cols):
        row = _mod_row(i, base, tps)
        y = acc + ex[2][:, cols] if has_bias else acc
        out[0][:, cols] = ex[0][:, cols] + ex[1][pl.ds(row, 1), cols] * y

    extras = [(res, (tm, tn), lambda g, i, j: (i, j)),
              (gate, (MOD_ROWS, tn), lambda g, i, j: (0, j))]
    if has_bias:
        extras.append((bias.reshape(1, N), (1, tn), lambda g, i, j: (0, j)))
    return _mm(a, w, epi, [_out2d(M, N, F32, tm, tn)], extras, tm=tm, tn=tn, tk=tk,
               wsel=wsel, name=name)[0]


def _ada_table(cc, ada_a, ada_b, ada_bias):
    L, D, R = ada_a.shape
    NM = ada_b.shape[-1]
    tn = _tile(NM, 4096, LANES)

    def body(c_ref, a_ref, b_ref, bias_ref, o_ref):
        c = c_ref[...]
        s = (c * (1.0 / (1.0 + jnp.exp(-c)))).astype(BF16)
        t = jnp.dot(s, a_ref[...].astype(BF16), preferred_element_type=F32)
        o = jnp.dot(t.astype(BF16), b_ref[...].astype(BF16), preferred_element_type=F32)
        o_ref[...] = o + bias_ref[...]

    return pl.pallas_call(
        body,
        grid=(L, NM // tn),
        in_specs=[pl.BlockSpec((MOD_ROWS, D), lambda l, j: (0, 0)),
                  pl.BlockSpec((None, D, R), lambda l, j: (l, 0, 0)),
                  pl.BlockSpec((None, R, tn), lambda l, j: (l, 0, j)),
                  pl.BlockSpec((None, 1, tn), lambda l, j: (l, 0, j))],
        out_specs=pl.BlockSpec((None, MOD_ROWS, tn), lambda l, j: (l, 0, j)),
        out_shape=jax.ShapeDtypeStruct((L, MOD_ROWS, NM), F32),
        compiler_params=_params(("parallel", "parallel")),
        name="ada_table",
    )(cc, ada_a, ada_b, ada_bias.reshape(L, 1, NM))


def _modulate(x, g, shift, scale, *, base, seq, name):
    M, D = x.shape
    tm = _tile(min(M, seq), 512, 16)
    tps = max(seq // tm, 1) if base == 0 else M // tm

    def body(x_ref, g_ref, sh_ref, sc_ref, o_ref):
        row = _mod_row(pl.program_id(0), base, tps)
        y = _rms(x_ref[...], D) * g_ref[...]
        o_ref[...] = (y * (1.0 + sc_ref[pl.ds(row, 1), :]) + sh_ref[pl.ds(row, 1), :]).astype(BF16)

    return pl.pallas_call(
        body,
        grid=(M // tm,),
        in_specs=[pl.BlockSpec((tm, D), lambda i: (i, 0)),
                  pl.BlockSpec((1, D), lambda i: (0, 0)),
                  pl.BlockSpec((MOD_ROWS, D), lambda i: (0, 0)),
                  pl.BlockSpec((MOD_ROWS, D), lambda i: (0, 0))],
        out_specs=pl.BlockSpec((tm, D), lambda i: (i, 0)),
        out_shape=jax.ShapeDtypeStruct((M, D), BF16),
        compiler_params=_params(("parallel",)),
        name=name,
    )(x, g.reshape(1, D), shift, scale)


def _ffn_up(h, wg, wv, cw, ghalo, layer, *, tm, tf, Fp, seq, name):
    M, K = h.shape
    F = wg.shape[-1]
    nm, nf = M // tm, Fp // tf
    inner = tm > seq
    assert seq % tm == 0 or tm % seq == 0
    last_valid = F - (nf - 1) * tf
    assert 0 < last_valid <= tf

    eg = 16

    def body(h_ref, wg_ref, wv_ref, cw_ref, halo_ref, u_ref):
        def tile_body(valid):
            cf = _tile(tf, 512 if valid == tf else 256, LANES)
            hb = h_ref[...]
            if inner:
                pos = lax.rem(lax.broadcasted_iota(jnp.int32, (tm, 1), 0), seq)
                seq_first, seq_last = pos == 0, pos == seq - 1
            edge_row = lax.broadcasted_iota(jnp.int32, (eg, 1), 0)
            for c in range(tf // cf):
                cs = slice(c * cf, (c + 1) * cf)
                if c * cf >= valid:
                    u_ref[:, cs] = jnp.zeros((tm, cf), BF16)
                    continue
                taps = cw_ref[:, cs]
                col_ok = None
                if (c + 1) * cf > valid:
                    col_ok = lax.broadcasted_iota(jnp.int32, (1, cf), 1) < valid - c * cf

                def gated(gp, g, gn, v):
                    conv = gp * taps[0:1, :] + g * taps[1:2, :] + gn * taps[2:3, :] + taps[3:4, :]
                    u = conv * (1.0 / (1.0 + jnp.exp2(conv * -LOG2E))) * v
                    if col_ok is not None:
                        u = jnp.where(col_ok, u, 0.0)
                    return u.astype(BF16)

                g = jnp.dot(hb, wg_ref[:, cs], preferred_element_type=F32)
                v = jnp.dot(hb, wv_ref[:, cs], preferred_element_type=F32)
                gp = pltpu.roll(g, 1, 0)
                gn = pltpu.roll(g, tm - 1, 0)
                if inner:
                    gp = jnp.where(seq_first, 0.0, gp)
                    gn = jnp.where(seq_last, 0.0, gn)
                u_ref[:, cs] = gated(gp, g, gn, v)
                top, bot = slice(0, eg), slice(tm - eg, tm)
                gp_top = jnp.where(edge_row == 0, halo_ref[0:1, cs], gp[top])
                u_ref[top, cs] = gated(gp_top, g[top], gn[top], v[top])
                gn_bot = jnp.where(edge_row == eg - 1, halo_ref[1:2, cs], gn[bot])
                u_ref[bot, cs] = gated(gp[bot], g[bot], gn_bot, v[bot])

        if last_valid == tf:
            tile_body(tf)
        else:
            j = pl.program_id(1)
            pl.when(j < nf - 1)(lambda: tile_body(tf))
            pl.when(j == nf - 1)(lambda: tile_body(last_valid))

    return pl.pallas_call(
        body,
        grid=(nm, nf),
        in_specs=[pl.BlockSpec((tm, K), lambda i, j: (i, 0)),
                  pl.BlockSpec((None, K, tf), lambda i, j: (layer, 0, j)),
                  pl.BlockSpec((None, K, tf), lambda i, j: (layer, 0, j)),
                  pl.BlockSpec((None, 8, tf), lambda i, j: (layer, 0, j)),
                  pl.BlockSpec((None, HALO_ROWS, tf), lambda i, j: (i, 0, j))],
        out_specs=pl.BlockSpec((tm, tf), lambda i, j: (i, j)),
        out_shape=jax.ShapeDtypeStruct((M, Fp), BF16),
        compiler_params=_params(("parallel", "parallel")),
        name=name,
    )(h, wg, wv, cw, ghalo)


def _ffn_weights(w_gate, w_val, conv_w, conv_b, w_down):
    F = w_gate.shape[-1]
    tf = 512 if F > 512 else F
    Fp = -(-F // tf) * tf
    cw = jnp.pad(jnp.concatenate([conv_w, conv_b[:, None, :]], axis=1),
                 ((0, 0), (0, 4), (0, Fp - F)))
    return dict(wg=w_gate.astype(BF16), wv=w_val.astype(BF16), wd=w_down.astype(BF16), cw=cw,
                tf=tf, Fp=Fp)


def _conv_ffn(h, x_res, gate, fw, layer, *, base, seq, name):
    M, D = h.shape
    Fp, tf = fw["Fp"], fw["tf"]
    tm = _tile(M, 1024, 16)
    nm = M // tm
    idx = np.zeros((nm, HALO_ROWS), np.int32)
    ok = np.zeros((nm, HALO_ROWS), bool)
    starts = np.arange(nm) * tm
    idx[:, 0], ok[:, 0] = np.maximum(starts - 1, 0), starts % seq != 0
    idx[:, 1], ok[:, 1] = np.minimum(starts + tm, M - 1), (starts + tm) % seq != 0
    h_halo = jnp.where(jnp.asarray(ok.reshape(-1, 1)),
                       jnp.take(h, jnp.asarray(idx.reshape(-1)), axis=0), 0).astype(BF16)

    def epi_plain(acc, ex, out, g, i, j, cols):
        out[0][:, cols] = acc

    mh = nm * HALO_ROWS
    ghalo = _mm(h_halo, fw["wg"], epi_plain, [_out2d(mh, Fp, F32, mh, tf)],
                tm=mh, tn=tf, tk=D, wsel=layer, n=Fp, name=name + "_halo")[0]
    u = _ffn_up(h, fw["wg"], fw["wv"], fw["cw"], ghalo.reshape(nm, HALO_ROWS, Fp), layer,
                tm=tm, tf=tf, Fp=Fp, seq=seq, name=name + "_up")
    return _mm_residual(u, fw["wd"], layer, x_res, gate, None, base=base, seq=seq,
                        tk=Fp // 4 if Fp % (4 * LANES) == 0 else Fp, name=name + "_down")


def _rope_tables(n_tok):
    t = jnp.arange(n_tok)
    inv = ROPE_THETA ** (-jnp.arange(ROPE_FREQS, dtype=F32) / ROPE_FREQS)
    row = (t // GRID_W).astype(F32)[:, None] * inv
    col = (t % GRID_W).astype(F32)[:, None] * inv
    z = jnp.zeros((n_tok, LANES - ROPE_DIM), F32)
    cos = jnp.concatenate([jnp.cos(row), jnp.cos(row), jnp.cos(col), jnp.cos(col), z], axis=1)
    sin = jnp.concatenate([-jnp.sin(row), jnp.sin(row), -jnp.sin(col), jnp.sin(col), z], axis=1)
    return cos, sin


ROPE_PARTNER = np.array([l + ROPE_FREQS if l % (2 * ROPE_FREQS) < ROPE_FREQS else l - ROPE_FREQS
                         for l in range(ROPE_DIM)])


def _rope(x, cos, sin):
    return x * cos + pltpu.roll(x, ROPE_DIM, 1) * sin


def _rope_cols(w):
    return jnp.concatenate([w, w[..., ROPE_PARTNER]], axis=-1)


def _rope_gains(g, scale=1.0):
    g = g.astype(F32) * scale
    return (_rope_cols(g)[:, None, :],
            jnp.concatenate([g, jnp.zeros_like(g)], axis=-1)[:, None, :])


def _head_major(M, seq, heads, width, tm, tn):
    tps = seq // tm
    return ((M // seq, heads, seq, width), BF16, (None, tn // width, tm, width),
            lambda g, i, jj: (i // tps, jj, i % tps, 0))


def _mla_keys(h, p, j, rope, *, seq, name):
    M, D = h.shape
    kv_rank = p["kv_g"].shape[-1]
    tm = _tile(min(M, seq), 1024, 16)
    tps = seq // tm
    use_rope = rope is not None

    def epi_dkv(acc, ex, out, g, i, jj, cols):
        out[0][...] = (_rms(acc[:, :kv_rank], kv_rank) * ex[0][...]).astype(BF16)
        r = _rms(acc[:, kv_rank:], 2 * ROPE_DIM) * ex[1][...]
        if use_rope:
            r = _rope(r, ex[2][...], ex[3][...])
        out[1][...] = r.astype(BF16)

    extras = [(p["kv_g"], (None, 1, kv_rank), lambda g, i, jj: (j, 0, 0)),
              (p["kr_g"][0 if use_rope else 1], (None, 1, LANES), lambda g, i, jj: (j, 0, 0))]
    if use_rope:
        extras += [(t, (tm, LANES), lambda g, i, jj: (i % tps, 0)) for t in rope]
    ckv, kr = _mm(h, p["w_dkv"], epi_dkv,
                  [_out2d(M, kv_rank, BF16, tm, kv_rank), _out2d(M, LANES, BF16, tm, LANES)],
                  extras, tm=tm, tn=kv_rank + LANES, tk=D, wsel=j, cn=None, name=name + "_dkv")

    HN = p["w_uk"].shape[-1]
    heads = HN // NOPE_DIM
    tn = _tile(HN, 2048, LANES)

    def epi_kn(acc, ex, out, g, i, jj, cols):
        for c in range(acc.shape[1] // NOPE_DIM):
            cs = slice(c * NOPE_DIM, (c + 1) * NOPE_DIM)
            out[0][cols.start // NOPE_DIM + c] = (
                _rms(acc[:, cs], NOPE_DIM) * ex[0][...]).astype(BF16)

    kn = _mm(ckv, p["w_uk"], epi_kn, [_head_major(M, seq, heads, NOPE_DIM, tm, tn)],
             [(p["kn_g"], (None, 1, NOPE_DIM), lambda g, i, jj: (j, 0, 0))],
             tm=tm, tn=tn, tk=kv_rank, wsel=j, name=name + "_uk")[0]

    def epi_cast(acc, ex, out, g, i, jj, cols):
        for c in range(acc.shape[1] // V_DIM):
            out[0][cols.start // V_DIM + c] = acc[:, c * V_DIM:(c + 1) * V_DIM].astype(BF16)

    v = _mm(ckv, p["w_uv"], epi_cast, [_head_major(M, seq, heads, V_DIM, tm, tn)],
            tm=tm, tn=tn, tk=kv_rank, wsel=j, name=name + "_uv")[0]
    return kn, kr.reshape(M // seq, seq, LANES), v


def _mla_queries(h, p, j, rope, *, seq, name):
    M, D = h.shape
    q_rank = p["q_g"].shape[-1]
    tm = _tile(min(M, seq), 512, 16)
    tps = seq // tm
    use_rope = rope is not None

    def epi_cq(acc, ex, out, g, i, jj, cols):
        out[0][...] = (_rms(acc, q_rank) * ex[0][...]).astype(BF16)

    cq = _mm(h, p["w_dq"], epi_cq, [_out2d(M, q_rank, BF16, tm, q_rank)],
             [(p["q_g"], (None, 1, q_rank), lambda g, i, jj: (j, 0, 0))],
             tm=tm, tn=q_rank, tk=D, wsel=j, cn=None, name=name + "_dq")[0]

    HQ = p["w_uq"].shape[-1]
    tn = _tile(HQ, 2048, HEAD_PAD)
    tm = _tile(min(M, seq), 1024, 16)
    tps = seq // tm

    def epi_q(acc, ex, out, g, i, jj, cols):
        for c in range(acc.shape[1] // HEAD_PAD):
            lo, hd = c * HEAD_PAD, cols.start // HEAD_PAD + c
            qn = _rms(acc[:, lo:lo + NOPE_DIM], NOPE_DIM) * ex[0][...]
            out[0][hd, :, 0:NOPE_DIM] = qn.astype(BF16)
            qr = _rms(acc[:, lo + NOPE_DIM:lo + HEAD_PAD], 2 * ROPE_DIM) * ex[1][...]
            if use_rope:
                qr = _rope(qr, ex[2][...], ex[3][...])
            out[0][hd, :, NOPE_DIM:HEAD_PAD] = qr.astype(BF16)

    extras = [(p["qn_g"], (None, 1, NOPE_DIM), lambda g, i, jj: (j, 0, 0)),
              (p["qr_g"][0 if use_rope else 1], (None, 1, LANES), lambda g, i, jj: (j, 0, 0))]
    if use_rope:
        extras += [(t, (tm, LANES), lambda g, i, jj: (i % tps, 0)) for t in rope]
    return _mm(cq, p["w_uq"], epi_q, [_head_major(M, seq, HQ // HEAD_PAD, HEAD_PAD, tm, tn)],
               extras, tm=tm, tn=tn, tk=q_rank, wsel=j, cn=HEAD_PAD, name=name + "_uq")[0]


def _attention(q, segs, *, heads, name):
    B, _, Nq, _ = q.shape
    lens = [s[1].shape[1] for s in segs]
    Nk = sum(lens)
    tq = _tile(Nq, 1024, 16)
    ck = min(512, Nk)
    chunks = [(o, min(ck, Nk - o)) for o in range(0, Nk, ck)]
    n_seg = len(segs)

    hq = tq // 2

    def body(q_ref, *refs):
        seg_refs = refs[:3 * n_seg]
        o_ref, k_buf, v_buf, s_buf = refs[3 * n_seg:]

        lane = lax.broadcasted_iota(jnp.int32, (Nk, LANES), 1)
        v_buf[:, V_DIM:] = jnp.where(lane == 0, 1.0, 0.0).astype(BF16)
        off = 0
        for s in range(n_seg):
            kn_ref, kr_ref, v_ref = seg_refs[3 * s:3 * s + 3]
            k_buf[off:off + lens[s], 0:NOPE_DIM] = kn_ref[...]
            k_buf[off:off + lens[s], NOPE_DIM:HEAD_PAD] = kr_ref[...]
            v_buf[off:off + lens[s], 0:V_DIM] = v_ref[...]
            off += lens[s]

        def q_block(qi, carry):
            q0 = pl.multiple_of(qi * tq, tq)
            mx = [None, None]
            for off, n in chunks:
                for hh in range(2):
                    s = lax.dot_general(q_ref[pl.ds(q0 + hh * hq, hq), :], k_buf[off:off + n, :],
                                        (((1,), (1,)), ((), ())), preferred_element_type=F32)
                    s_buf[hh * hq:(hh + 1) * hq, off:off + n] = s
                    for t in range(n // LANES):
                        st = s[:, t * LANES:(t + 1) * LANES]
                        mx[hh] = st if mx[hh] is None else jnp.maximum(mx[hh], st)
            for hh in range(2):
                m = jnp.max(mx[hh], axis=-1, keepdims=True)
                p = jnp.exp2(s_buf[hh * hq:(hh + 1) * hq, :] - m).astype(BF16)
                ov = jnp.dot(p, v_buf[...], preferred_element_type=F32)
                o_ref[pl.ds(q0 + hh * hq, hq), :] = (
                    ov[:, :V_DIM] / ov[:, V_DIM:V_DIM + 1]).astype(BF16)
            return carry

        lax.fori_loop(0, Nq // tq, q_block, 0)

    in_specs = [pl.BlockSpec((None, None, Nq, HEAD_PAD), lambda b, h: (b, h, 0, 0))]
    args = [q]
    for (kn, kr, v), n in zip(segs, lens):
        in_specs += [pl.BlockSpec((None, None, n, NOPE_DIM), lambda b, h: (b, h, 0, 0)),
                     pl.BlockSpec((None, n, LANES), lambda b, h: (b, 0, 0)),
                     pl.BlockSpec((None, None, n, V_DIM), lambda b, h: (b, h, 0, 0))]
        args += [kn, kr, v]
    return pl.pallas_call(
        body,
        grid=(B, heads),
        in_specs=in_specs,
        out_specs=pl.BlockSpec((None, Nq, V_DIM), lambda b, h: (b, 0, h)),
        out_shape=jax.ShapeDtypeStruct((B, Nq, heads * V_DIM), BF16),
        scratch_shapes=[pltpu.VMEM((Nk, HEAD_PAD), BF16), pltpu.VMEM((Nk, V_DIM + LANES), BF16),
                        pltpu.VMEM((tq, Nk), F32)],
        compiler_params=_params(("parallel", "parallel")),
        name=name,
    )(*args)


def _mla_weights(w_dq, q_norm_g, w_uq, w_dkv, kv_norm_g, w_ukv, qn_nope_g, qn_rope_g,
                 kn_nope_g, kn_rope_g, w_o):
    L, _, q_rank = w_dq.shape
    kv_rank = kv_norm_g.shape[-1]
    heads = w_uq.shape[-1] // (NOPE_DIM + ROPE_DIM)
    w_uq_h = w_uq.astype(BF16).reshape(L, q_rank, heads, NOPE_DIM + ROPE_DIM)
    w_uq_p = jnp.concatenate([w_uq_h[..., :NOPE_DIM], _rope_cols(w_uq_h[..., NOPE_DIM:])], axis=-1)
    w_dkv = w_dkv.astype(BF16)
    w_ukv_h = w_ukv.astype(BF16).reshape(L, kv_rank, heads, NOPE_DIM + V_DIM)
    qscale = float(NOPE_DIM + ROPE_DIM) ** -0.5 * LOG2E
    return heads, {
        "w_dq": w_dq.astype(BF16),
        "q_g": q_norm_g[:, None, :],
        "w_uq": w_uq_p.reshape(L, q_rank, heads * HEAD_PAD),
        "w_dkv": jnp.concatenate([w_dkv[..., :kv_rank], _rope_cols(w_dkv[..., kv_rank:])], axis=-1),
        "kv_g": kv_norm_g[:, None, :],
        "w_uk": w_ukv_h[..., :NOPE_DIM].reshape(L, kv_rank, heads * NOPE_DIM),
        "w_uv": w_ukv_h[..., NOPE_DIM:].reshape(L, kv_rank, heads * V_DIM),
        "qn_g": qn_nope_g[:, None, :] * qscale,
        "qr_g": _rope_gains(qn_rope_g, qscale),
        "kn_g": kn_nope_g[:, None, :],
        "kr_g": _rope_gains(kn_rope_g),
        "w_o": w_o.astype(BF16),
    }


def _mla_mixer(hl, hc, rope, heads, p, j, B, ctx_out, tag):
    S, C = hl.shape[0] // B, hc.shape[0] // B
    seg_c = _mla_keys(hc, p, j, None, seq=C, name=tag + "_kc")
    seg_l = _mla_keys(hl, p, j, rope, seq=S, name=tag + "_kl")
    q_l = _mla_queries(hl, p, j, rope, seq=S, name=tag + "_ql")
    ol = _attention(q_l, [seg_c, seg_l], heads=heads, name=tag + "_attn_l")
    oc = None
    if ctx_out:
        q_c = _mla_queries(hc, p, j, None, seq=C, name=tag + "_qc")
        oc = _attention(q_c, [seg_c], heads=heads, name=tag + "_attn_c")
        oc = oc.reshape(B * C, heads * V_DIM)
    return ol.reshape(B * S, heads * V_DIM), oc


def _gelu_tanh(x):
    return x * (0.5 * (1.0 + jnp.tanh(0.7978845608028654 * (x + 0.044715 * (x * x * x)))))


def _gm_spatial(v, u, ln_g, ln_b, w_s, b_s_t, *, name):
    M, W = v.shape
    G, C, _ = w_s.shape
    wg = W // G
    tm = _tile(M, 2 * C, C)

    def body(v_ref, u_ref, g_ref, b_ref, ws_ref, bs_ref, o_ref):
        x = v_ref[...]
        d = x - jnp.mean(x, axis=-1, keepdims=True)
        y = d * lax.rsqrt(jnp.mean(d * d, axis=-1, keepdims=True) + NORM_EPS)
        y = (y * g_ref[...] + b_ref[...]).astype(BF16)
        for c in range(tm // C):
            rs = slice(c * C, (c + 1) * C)
            for g in range(G):
                cs = slice(g * wg, (g + 1) * wg)
                sv = jnp.dot(ws_ref[g], y[rs, cs], preferred_element_type=F32) + bs_ref[:, g:g + 1]
                o_ref[rs, cs] = (u_ref[rs, cs].astype(F32) * sv).astype(BF16)

    return pl.pallas_call(
        body,
        grid=(M // tm,),
        in_specs=[pl.BlockSpec((tm, W), lambda i: (i, 0)),
                  pl.BlockSpec((tm, W), lambda i: (i, 0)),
                  pl.BlockSpec((1, W), lambda i: (0, 0)),
                  pl.BlockSpec((1, W), lambda i: (0, 0)),
                  pl.BlockSpec((G, C, C), lambda i: (0, 0, 0)),
                  pl.BlockSpec((C, G), lambda i: (0, 0))],
        out_specs=pl.BlockSpec((tm, W), lambda i: (i, 0)),
        out_shape=jax.ShapeDtypeStruct((M, W), BF16),
        compiler_params=_params(("parallel",)),
        name=name,
    )(v, u, ln_g.reshape(1, W), ln_b.reshape(1, W), w_s, b_s_t)


def _gmlp_mixer(h, w_in, j, b_in, v_norm_g, v_norm_b, w_s, b_s, tag):
    M, D = h.shape
    W = w_in.shape[-1] // 2
    tm, tn = _tile(M, 1024, 16), _tile(W, 1024, LANES)

    def epi_gelu(dtype):
        def epi(acc, ex, out, g, i, jj, cols):
            out[0][:, cols] = _gelu_tanh(acc + ex[0][:, cols]).astype(dtype)
        return epi

    def half(col0, b, dtype, name):
        return _mm(h, w_in, epi_gelu(dtype), [_out2d(M, W, dtype, tm, tn)],
                   [(b.reshape(1, W), (1, tn), lambda g, i, jj: (0, jj))],
                   tm=tm, tn=tn, tk=D, wsel=j, wcol0=col0, n=W, name=name)[0]

    u = half(0, b_in[:W], BF16, tag + "_in_u")
    v = half(W // tn, b_in[W:], F32, tag + "_in_v")
    return _gm_spatial(v, u, v_norm_g, v_norm_b, w_s.astype(BF16), b_s.T, name=tag + "_spatial")


def _dft_tables(n, k=None, ncols=None):
    ncols = n if ncols is None else ncols
    r = 64 if ncols % 64 == 0 and n % 64 == 0 and ncols > 64 else 1
    k = (jnp.arange(n, dtype=jnp.int32) if k is None else k)[:, None]

    def table(j, period):
        ang = ((k * j[None, :]) % period).astype(F32) * (2.0 * np.pi / period)
        return jnp.cos(ang), jnp.sin(ang)

    if r == 1:
        return table(jnp.arange(ncols, dtype=jnp.int32), n)
    cb, sb = table(jnp.arange(r, dtype=jnp.int32), n)
    ca, sa = table(jnp.arange(ncols // r, dtype=jnp.int32), n // r)
    ca, sa, cb, sb = ca[:, :, None], sa[:, :, None], cb[:, None, :], sb[:, None, :]
    shape = (k.shape[0], ncols)
    return (ca * cb - sa * sb).reshape(shape), (sa * cb + ca * sb).reshape(shape)


def _dft_channels(h, w1, B, *, name):
    M, D = h.shape
    n = M // B
    nh = n // 2
    dg = w1.shape[0]
    tm = _tile(nh, 1024, 16)
    tpb = nh // tm

    def body(lo_ref, hi_ref, w_ref, o_ref):
        lo = jnp.dot(lo_ref[...], w_ref[...], preferred_element_type=F32)
        hi = jnp.dot(hi_ref[...], w_ref[...], preferred_element_type=F32)
        for parity, r in enumerate((lo + hi, lo - hi)):
            o_ref[parity, 0] = r[:, :dg].astype(BF16)
            o_ref[parity, 1] = r[:, dg:].astype(BF16)

    return pl.pallas_call(
        body,
        grid=(B * tpb, D // dg),
        in_specs=[pl.BlockSpec((tm, dg), lambda i, g: ((i // tpb) * 2 * tpb + i % tpb, g)),
                  pl.BlockSpec((tm, dg), lambda i, g: ((i // tpb) * 2 * tpb + tpb + i % tpb, g)),
                  pl.BlockSpec((dg, 2 * dg), lambda i, g: (0, 0))],
        out_specs=pl.BlockSpec((None, 2, 2, tm, dg), lambda i, g: (i // tpb, 0, 0, i % tpb, g)),
        out_shape=jax.ShapeDtypeStruct((B, 2, 2, nh, D), BF16),
        compiler_params=_params(("parallel", "parallel")),
        name=name,
    )(h, h, w1)


def _fourier_mixer(h, B, tag):
    M, D = h.shape
    n = M // B
    nh = n // 2
    dg = D // FN_GROUPS
    cd, sd = _dft_tables(dg)
    w1 = jnp.concatenate([cd, sd], axis=1).astype(BF16)
    freq = 2 * jnp.arange(nh, dtype=jnp.int32)
    a2 = []
    for p in range(2):
        cn, sn = _dft_tables(n, k=freq + p, ncols=nh)
        a2.append(jnp.concatenate([cn, -sn], axis=1))
    a2 = jnp.stack(a2).astype(BF16)
    norm = float(n * dg) ** -0.5
    xcs = _dft_channels(h, w1, B, name=tag + "_dft_ch")

    def epi_scale(acc, ex, out, g, i, j, cols):
        out[0][:, cols] = (acc * norm).astype(BF16)

    tm2, tn2, tk2 = _tile(nh, 1024, 16), _tile(D, 1024, LANES), _tile(n, 2048, LANES)
    f = _mm(a2, xcs.reshape(B * 2, n, D), epi_scale,
            [((B, nh, 2 * D), BF16, (None, tm2, tn2),
              lambda g, i, j: (g // 2, i, (g % 2) * (D // tn2) + j))],
            tm=tm2, tn=tn2, tk=tk2, name=tag + "_dft_seq")[0]
    return f.reshape(M, D)


def kernel(x, c, ctx, c_ctx, norm1_g, norm2_g, ada_a, ada_b, ada_bias, ffn_w_gate, ffn_w_val, ffn_conv_w, ffn_conv_b, ffn_w_down, mla_w_dq, mla_q_norm_g, mla_w_uq, mla_w_dkv, mla_kv_norm_g, mla_w_ukv, mla_qn_nope_g, mla_qn_rope_g, mla_kn_nope_g, mla_kn_rope_g, mla_w_o, gm_w_in, gm_b_in, gm_v_norm_g, gm_v_norm_b, gm_w_s, gm_b_s, gm_w_out, fn_w_f, fn_b_f):
    B, S, D = x.shape
    C = ctx.shape[1]
    depth = norm1_g.shape[0]
    assert B < MOD_ROWS
    ctx_row = B

    cc = jnp.zeros((MOD_ROWS, D), F32).at[:B].set(c).at[ctx_row].set(c_ctx)
    mods = _ada_table(cc, ada_a, ada_b, ada_bias)
    rope = _rope_tables(S)
    fw = _ffn_weights(ffn_w_gate, ffn_w_val, ffn_conv_w, ffn_conv_b, ffn_w_down)
    heads, mp = _mla_weights(mla_w_dq, mla_q_norm_g, mla_w_uq, mla_w_dkv, mla_kv_norm_g,
                             mla_w_ukv, mla_qn_nope_g, mla_qn_rope_g, mla_kn_nope_g,
                             mla_kn_rope_g, mla_w_o)
    gm_in, gm_out, fn_w = gm_w_in.astype(BF16), gm_w_out.astype(BF16), fn_w_f.astype(BF16)

    xl = x.reshape(B * S, D)
    xc = ctx.reshape(B * C, D)
    lat = dict(base=0, seq=S)
    cx = dict(base=ctx_row, seq=C)

    for i in range(depth):
        last = i == depth - 1
        kind, j = i % N_MIXERS, i // N_MIXERS
        sh1, sc1, g1, sh2, sc2, g2 = (mods[i, :, k * D:(k + 1) * D] for k in range(N_MOD))
        tag = "l%d" % i
        hl = _modulate(xl, norm1_g[i], sh1, sc1, name=tag + "_mod1_l", **lat)
        need_ctx = not last
        hc = None
        if need_ctx or kind == 0:
            hc = _modulate(xc, norm1_g[i], sh1, sc1, name=tag + "_mod1_c", **cx)

        if kind == 0:
            yl, yc = _mla_mixer(hl, hc, rope, heads, mp, j, B, need_ctx, tag + "_mla")
            w_mix, b_mix, mix_tk = mp["w_o"], None, 2048
        elif kind == 1:
            gm = (gm_in, j, gm_b_in[j], gm_v_norm_g[j], gm_v_norm_b[j], gm_w_s[j], gm_b_s[j])
            yl = _gmlp_mixer(hl, *gm, tag + "_gm_l")
            yc = _gmlp_mixer(hc, *gm, tag + "_gm_c") if need_ctx else None
            w_mix, b_mix, mix_tk = gm_out, None, 4096
        else:
            yl = _fourier_mixer(hl, B, tag + "_fn_l")
            yc = _fourier_mixer(hc, B, tag + "_fn_c") if need_ctx else None
            w_mix, b_mix, mix_tk = fn_w, fn_b_f[j], 4096

        mix = dict(tn=512 if mix_tk == 4096 else 1024, tk=mix_tk)
        xl = _mm_residual(yl, w_mix, j, xl, g1, b_mix, name=tag + "_mix_l", **mix, **lat)
        h2 = _modulate(xl, norm2_g[i], sh2, sc2, name=tag + "_mod2_l", **lat)
        xl = _conv_ffn(h2, xl, g2, fw, i, name=tag + "_ffn_l", **lat)
        if need_ctx:
            xc = _mm_residual(yc, w_mix, j, xc, g1, b_mix, name=tag + "_mix_c", **mix, **cx)
            h2 = _modulate(xc, norm2_g[i], sh2, sc2, name=tag + "_mod2_c", **cx)
            xc = _conv_ffn(h2, xc, g2, fw, i, name=tag + "_ffn_c", **cx)
    return xl.reshape(B, S, D)
```
